```python
import math
import jax, jax.numpy as jnp
from jax import lax
import numpy as np

D_MODEL = 4096
BATCH = 4
SEQ = 4096
DEPTH = 2
DEC_BATCH = 16
DEC_SEQ = 64
PAST_LEN = 2048

CHUNK = 64
Q_BLOCK = 128
PLE_DIM = 256
S5_WIDTH = D_MODEL // 4
S5_GROUP = 16
S5_GROUPS = S5_WIDTH // S5_GROUP
S5_STATE = 64
HEAD_DIM = 128
ATT_WIDTH = D_MODEL // 2
N_HEADS = ATT_WIDTH // (2 * HEAD_DIM)
LRU_WIDTH = D_MODEL // 4
LRU_BLOCKS = 16
LRU_BLOCK = LRU_WIDTH // LRU_BLOCKS
CONV_WIDTH = 4
LRU_C = 8.0
D_FF = 11008
N_BRANCHES = 3
IN_COLS = S5_WIDTH + 3 * ATT_WIDTH + 2 * LRU_WIDTH + N_BRANCHES * D_MODEL
DEEPNORM_ALPHA = (2 * DEPTH) ** 0.25
DEEPNORM_BETA = (8 * DEPTH) ** -0.25
LN_EPS = 1e-5
NEG_INF = -1e30

kernel_name = "hybrid_streaming_s5_diffattn_rglru"


def _layer_norm(x, g, b):
    xf = x.astype(jnp.float32)
    mu = jnp.mean(xf, -1, keepdims=True)
    var = jnp.mean(jnp.square(xf - mu), -1, keepdims=True)
    y = (xf - mu) * lax.rsqrt(var + LN_EPS) * g.astype(jnp.float32) + b.astype(jnp.float32)
    return y.astype(x.dtype)


def _swiglu(x, w_in, w_out):
    gate, up = jnp.split(x @ w_in, 2, axis=-1)
    return (jax.nn.silu(gate) * up) @ w_out


def _linear_scan(a, b, h0):
    b = b.at[:, 0].add(a[:, 0] * h0)
    def combine(l, r):
        return (l[0] * r[0], r[0] * l[1] + r[1])
    _, h = lax.associative_scan(combine, (a, b), axis=1)
    return h


def _s5_branch(u, h0_re, h0_im, lam_re, lam_im, log_step, b_re, b_im, c_re, c_im, d_skip, w_glu, b_glu):
    f32 = jnp.float32
    bsz, t, _ = u.shape
    uf = u.astype(f32)
    lam = lax.complex(lam_re.astype(f32), lam_im.astype(f32))
    step = jnp.exp(log_step.astype(f32))[:, None]
    lam_bar = jnp.exp(lam * step)
    b_bar = ((lam_bar - 1.0) / lam)[..., None] * lax.complex(b_re.astype(f32), b_im.astype(f32))
    c_mat = lax.complex(c_re.astype(f32), c_im.astype(f32))
    ug = uf.reshape(bsz, t, S5_GROUPS, S5_GROUP).astype(jnp.complex64)
    bu = jnp.einsum('btgc,gpc->btgp', ug, b_bar)
    h0 = lax.complex(h0_re.astype(f32), h0_im.astype(f32))
    h = _linear_scan(jnp.broadcast_to(lam_bar, bu.shape), bu, h0)
    y = jnp.real(jnp.einsum('btgp,gcp->btgc', h, c_mat)).reshape(bsz, t, S5_WIDTH)
    y = jax.nn.gelu(y + d_skip.astype(f32) * uf)
    y = y * jax.nn.sigmoid(y @ w_glu.astype(f32) + b_glu.astype(f32))
    h_last = h[:, -1]
    return y.astype(u.dtype), jnp.real(h_last), jnp.imag(h_last)


def _rglru_branch(xr, gate_in, conv_buf, h0, conv_w, conv_b, w_a, b_a, w_x, b_x, lru_lambda):
    f32 = jnp.float32
    bsz, t, _ = xr.shape
    xpad = jnp.concatenate([conv_buf.astype(xr.dtype), xr], axis=1)
    xc = conv_b.astype(f32)
    for j in range(CONV_WIDTH):
        xc = xc + conv_w[j].astype(f32) * xpad[:, j:j + t].astype(f32)
    new_buf = xpad[:, t:]
    xb = xc.reshape(bsz, t, LRU_BLOCKS, LRU_BLOCK)
    r = jax.nn.sigmoid(jnp.einsum('btnc,ncd->btnd', xb, w_a.astype(f32)) + b_a.astype(f32)).reshape(bsz, t, LRU_WIDTH)
    ig = jax.nn.sigmoid(jnp.einsum('btnc,ncd->btnd', xb, w_x.astype(f32)) + b_x.astype(f32)).reshape(bsz, t, LRU_WIDTH)
    log_a = -LRU_C * r * jax.nn.softplus(-lru_lambda.astype(f32))
    a = jnp.exp(log_a)
    mult = jnp.sqrt(-jnp.expm1(2.0 * log_a))
    h = _linear_scan(a, mult * (ig * xc), h0.astype(f32))
    y = h * jax.nn.gelu(gate_in.astype(f32))
    return y.astype(xr.dtype), h[:, -1], new_buf


def _diff_attend(q, k, v, qpos, kpos, lam, slopes):
    f32 = jnp.float32
    s = jnp.einsum('bqhid,bkhid->bhiqk', q.astype(f32), k.astype(f32)) * (HEAD_DIM ** -0.5)
    dist = jnp.abs(qpos[:, None] - kpos[None, :]).astype(f32)
    s = s - slopes[None, :, None, None, None] * dist[None, None, None]
    visible = (kpos[None, :] // CHUNK) <= (qpos[:, None] // CHUNK)
    s = jnp.where(visible[None, None, None], s, NEG_INF)
    p = jax.nn.softmax(s, axis=-1)
    w = p[:, :, 0] - lam * p[:, :, 1]
    return jnp.einsum('bhqk,bkhe->bqhe', w, v.astype(f32))


def _diff_head_norm(o, g, lam_init):
    bsz, t = o.shape[0], o.shape[1]
    o = o * lax.rsqrt(jnp.mean(o * o, -1, keepdims=True) + LN_EPS) * g.astype(jnp.float32)
    return (o * (1.0 - lam_init)).reshape(bsz, t, ATT_WIDTH)


def setup_inputs(seed: int = 0) -> dict:
    key = jax.random.key(seed)
    ks = iter(jax.random.split(key, 48))
    f32 = jnp.float32
    def nrm(shape, scale):
        return jax.random.normal(next(ks), shape, f32) * scale
    def unif(shape, lo, hi):
        return jax.random.uniform(next(ks), shape, f32, lo, hi)
    lam_im0 = jnp.pi * jnp.arange(S5_STATE, dtype=f32)
    a_c = unif((DEPTH, LRU_WIDTH), 0.9, 0.999) ** (1.0 / LRU_C)
    return {
        "x_prompt": nrm((BATCH, SEQ, D_MODEL), 1.0),
        "x_sample": nrm((DEC_BATCH, DEC_SEQ, D_MODEL), 1.0),
        "cache_k": nrm((DEPTH, DEC_BATCH, PAST_LEN, N_HEADS, 2 * HEAD_DIM), 1.0),
        "cache_v": nrm((DEPTH, DEC_BATCH, PAST_LEN, N_HEADS, 2 * HEAD_DIM), 1.0),
        "state_s5_re": nrm((DEPTH, DEC_BATCH, S5_GROUPS, S5_STATE), 0.3),
        "state_s5_im": nrm((DEPTH, DEC_BATCH, S5_GROUPS, S5_STATE), 0.3),
        "state_lru": nrm((DEPTH, DEC_BATCH, LRU_WIDTH), 0.5),
        "state_conv": nrm((DEPTH, DEC_BATCH, CONV_WIDTH - 1, LRU_WIDTH), 1.0),
        "p_prompt": nrm((DEPTH, BATCH, SEQ, PLE_DIM), 1.0),
        "p_sample": nrm((DEPTH, DEC_BATCH, DEC_SEQ, PLE_DIM), 1.0),
        "ln_g": 1.0 + nrm((DEPTH, 3, D_MODEL), 0.02),
        "ln_b": nrm((DEPTH, 3, D_MODEL), 0.02),
        "ffn_w_in": nrm((DEPTH, 2, D_MODEL, 2 * D_FF), D_MODEL ** -0.5),
        "ffn_w_out": nrm((DEPTH, 2, D_FF, D_MODEL), D_FF ** -0.5 * DEEPNORM_BETA),
        "w_in": nrm((DEPTH, D_MODEL, IN_COLS), D_MODEL ** -0.5),
        "b_gate": nrm((DEPTH, N_BRANCHES, D_MODEL), 0.1),
        "s5_lam_re": -0.5 + nrm((DEPTH, S5_GROUPS, S5_STATE), 0.01),
        "s5_lam_im": lam_im0 + nrm((DEPTH, S5_GROUPS, S5_STATE), 0.01),
        "s5_log_step": unif((DEPTH, S5_GROUPS), math.log(1e-3), math.log(1e-1)),
        "s5_b_re": nrm((DEPTH, S5_GROUPS, S5_STATE, S5_GROUP), (2 * S5_GROUP) ** -0.5),
        "s5_b_im": nrm((DEPTH, S5_GROUPS, S5_STATE, S5_GROUP), (2 * S5_GROUP) ** -0.5),
        "s5_c_re": nrm((DEPTH, S5_GROUPS, S5_GROUP, S5_STATE), S5_STATE ** -0.5),
        "s5_c_im": nrm((DEPTH, S5_GROUPS, S5_GROUP, S5_STATE), S5_STATE ** -0.5),
        "s5_d": nrm((DEPTH, S5_WIDTH), 0.5),
        "s5_w_glu": nrm((DEPTH, S5_WIDTH, S5_WIDTH), S5_WIDTH ** -0.5),
        "s5_b_glu": nrm((DEPTH, S5_WIDTH), 0.01),
        "diff_lambda": nrm((DEPTH, 4, HEAD_DIM), 0.1),
        "diff_subln": 1.0 + nrm((DEPTH, 2 * HEAD_DIM), 0.02),
        "lru_conv_w": nrm((DEPTH, CONV_WIDTH, LRU_WIDTH), CONV_WIDTH ** -0.5),
        "lru_conv_b": nrm((DEPTH, LRU_WIDTH), 0.01),
        "lru_w_a": nrm((DEPTH, LRU_BLOCKS, LRU_BLOCK, LRU_BLOCK), LRU_BLOCK ** -0.5),
        "lru_b_a": nrm((DEPTH, LRU_BLOCKS, LRU_BLOCK), 0.01),
        "lru_w_x": nrm((DEPTH, LRU_BLOCKS, LRU_BLOCK, LRU_BLOCK), LRU_BLOCK ** -0.5),
        "lru_b_x": nrm((DEPTH, LRU_BLOCKS, LRU_BLOCK), 0.01),
        "lru_lambda": jnp.log(a_c) - jnp.log1p(-a_c),
        "w_br_a": nrm((DEPTH, S5_WIDTH, D_MODEL), S5_WIDTH ** -0.5 * DEEPNORM_BETA),
        "w_br_b": nrm((DEPTH, ATT_WIDTH, D_MODEL), ATT_WIDTH ** -0.5 * DEEPNORM_BETA),
        "w_br_c": nrm((DEPTH, LRU_WIDTH, D_MODEL), LRU_WIDTH ** -0.5 * DEEPNORM_BETA),
        "w_o": nrm((DEPTH, D_MODEL, D_MODEL), D_MODEL ** -0.5 * DEEPNORM_BETA),
        "ple_w_proj": nrm((DEPTH, PLE_DIM, D_MODEL), PLE_DIM ** -0.5),
        "ple_w_gate": nrm((DEPTH, D_MODEL, D_MODEL), D_MODEL ** -0.5),
    }


def reference(x_prompt, x_sample, cache_k, cache_v, state_s5_re, state_s5_im, state_lru, state_conv,
              p_prompt, p_sample, ln_g, ln_b, ffn_w_in, ffn_w_out, w_in, b_gate,
              s5_lam_re, s5_lam_im, s5_log_step, s5_b_re, s5_b_im, s5_c_re, s5_c_im, s5_d, s5_w_glu, s5_b_glu,
              diff_lambda, diff_subln, lru_conv_w, lru_conv_b, lru_w_a, lru_b_a, lru_w_x, lru_b_x, lru_lambda,
              w_br_a, w_br_b, w_br_c, w_o, ple_w_proj, ple_w_gate):
    f32 = jnp.float32
    slopes = jnp.exp2(-8.0 * (jnp.arange(N_HEADS, dtype=f32) + 1.0) / N_HEADS)
    splits = np.cumsum([S5_WIDTH, ATT_WIDTH, ATT_WIDTH, ATT_WIDTH, LRU_WIDTH, LRU_WIDTH]).tolist()

    def mixer(i, x, pos0, k_past, v_past, s5_re0, s5_im0, lru0, conv0):
        bsz, t, _ = x.shape
        proj = x @ w_in[i]
        u, q, k, v, xr, gr, gl = jnp.split(proj, splits, axis=-1)
        gates = jax.nn.sigmoid(gl.astype(f32).reshape(bsz, t, N_BRANCHES, D_MODEL) + b_gate[i].astype(f32))
        y_a, s5_re, s5_im = _s5_branch(u, s5_re0, s5_im0, s5_lam_re[i], s5_lam_im[i], s5_log_step[i],
                                       s5_b_re[i], s5_b_im[i], s5_c_re[i], s5_c_im[i], s5_d[i],
                                       s5_w_glu[i], s5_b_glu[i])
        lam_init = 0.8 - 0.6 * math.exp(-0.3 * i)
        dl = diff_lambda[i].astype(f32)
        lam = jnp.exp(jnp.sum(dl[0] * dl[1])) - jnp.exp(jnp.sum(dl[2] * dl[3])) + lam_init
        qh = q.reshape(bsz, t, N_HEADS, 2, HEAD_DIM)
        k_rows = k.reshape(bsz, t, N_HEADS, 2 * HEAD_DIM)
        v_rows = v.reshape(bsz, t, N_HEADS, 2 * HEAD_DIM)
        if k_past is None:
            k4 = k_rows.reshape(bsz, t, N_HEADS, 2, HEAD_DIM)
            kpos = jnp.arange(t)
            def block(bi):
                q0 = bi * Q_BLOCK
                qb = lax.dynamic_slice_in_dim(qh, q0, Q_BLOCK, axis=1)
                return _diff_attend(qb, k4, v_rows, q0 + jnp.arange(Q_BLOCK), kpos, lam, slopes)
            o = lax.map(block, jnp.arange(t // Q_BLOCK))
            o = jnp.moveaxis(o, 0, 1).reshape(bsz, t, N_HEADS, 2 * HEAD_DIM)
        else:
            k_all = jnp.concatenate([k_past.astype(k_rows.dtype), k_rows], axis=1)
            v_all = jnp.concatenate([v_past.astype(v_rows.dtype), v_rows], axis=1)
            tk = k_all.shape[1]
            o = _diff_attend(qh, k_all.reshape(bsz, tk, N_HEADS, 2, HEAD_DIM), v_all,
                             pos0 + jnp.arange(t), jnp.arange(tk), lam, slopes)
        y_b = _diff_head_norm(o, diff_subln[i], lam_init).astype(x.dtype)
        y_c, lru_h, conv_new = _rglru_branch(xr, gr, conv0, lru0, lru_conv_w[i], lru_conv_b[i],
                                             lru_w_a[i], lru_b_a[i], lru_w_x[i], lru_b_x[i], lru_lambda[i])
        merged = (gates[:, :, 0] * (y_a @ w_br_a[i]).astype(f32)
                  + gates[:, :, 1] * (y_b @ w_br_b[i]).astype(f32)
                  + gates[:, :, 2] * (y_c @ w_br_c[i]).astype(f32))
        out = merged.astype(x.dtype) @ w_o[i]
        return out, k_rows, v_rows, s5_re, s5_im, lru_h, conv_new

    def run_group(x, p, k_cache, v_cache, s5_re_in, s5_im_in, lru_in, conv_in, pos0):
        ks_, vs_, sre_, sim_, lh_, cb_ = [], [], [], [], [], []
        for i in range(DEPTH):
            x = _layer_norm(DEEPNORM_ALPHA * x + 0.5 * _swiglu(x, ffn_w_in[i, 0], ffn_w_out[i, 0]), ln_g[i, 0], ln_b[i, 0])
            kp = None if k_cache is None else k_cache[i]
            vp = None if v_cache is None else v_cache[i]
            mix, kr, vr, sr, si, lh, cb = mixer(i, x, pos0, kp, vp, s5_re_in[i], s5_im_in[i], lru_in[i], conv_in[i])
            x = _layer_norm(DEEPNORM_ALPHA * x + mix, ln_g[i, 1], ln_b[i, 1])
            x = _layer_norm(DEEPNORM_ALPHA * x + 0.5 * _swiglu(x, ffn_w_in[i, 1], ffn_w_out[i, 1]), ln_g[i, 2], ln_b[i, 2])
            x = x + jax.nn.sigmoid(x @ ple_w_gate[i]) * (p[i] @ ple_w_proj[i])
            ks_.append(kr); vs_.append(vr); sre_.append(sr); sim_.append(si); lh_.append(lh); cb_.append(cb)
        return (x, jnp.stack(ks_), jnp.stack(vs_), jnp.stack(sre_), jnp.stack(sim_), jnp.stack(lh_), jnp.stack(cb_))

    bp = x_prompt.shape[0]
    zero_s5 = jnp.zeros((DEPTH, bp, S5_GROUPS, S5_STATE), f32)
    zero_lru = jnp.zeros((DEPTH, bp, LRU_WIDTH), f32)
    zero_conv = jnp.zeros((DEPTH, bp, CONV_WIDTH - 1, LRU_WIDTH), x_prompt.dtype)
    y_prompt, k_p, v_p, sre_p, sim_p, lru_p, conv_p = run_group(
        x_prompt, p_prompt, None, None, zero_s5, zero_s5, zero_lru, zero_conv, 0)
    y_sample, k_s, v_s, sre_s, sim_s, lru_s, conv_s = run_group(
        x_sample, p_sample, cache_k, cache_v, state_s5_re, state_s5_im, state_lru, state_conv, PAST_LEN)
    return (y_prompt, y_sample, k_p, v_p, sre_p, sim_p, lru_p, conv_p, k_s, v_s, sre_s, sim_s, lru_s, conv_s)
```

```python
import functools
import math

import jax
import jax.numpy as jnp
from jax import lax
from jax.experimental import pallas as pl
from jax.experimental.pallas import tpu as pltpu

F32 = jnp.float32
BF16 = jnp.bfloat16

CHUNK = 64
LRU_C = 8.0
LN_EPS = 1e-5
NEG_INF = -1e30
S5_CHUNK = 64
LRU_ROWS = 64

MIB = 1024 * 1024


def _params(semantics, vmem_mib):
    return pltpu.CompilerParams(dimension_semantics=semantics, vmem_limit_bytes=vmem_mib * MIB)


def _dot(a, b):
    return jnp.dot(a, b, preferred_element_type=F32)


def _dot_nt(a, b):
    return lax.dot_general(a, b, (((1,), (1,)), ((), ())), preferred_element_type=F32)


def _split_bf16(a):
    hi = a.astype(BF16)
    lo = (a - hi.astype(F32)).astype(BF16)
    return hi, lo


def _dot3(a, b):
    a_hi, a_lo = _split_bf16(a)
    b_hi, b_lo = _split_bf16(b)
    return _dot(a_hi, b_hi) + (_dot(a_lo, b_hi) + _dot(a_hi, b_lo))


def _mm_kernel(x_ref, w_ref, o_ref, *, nk):
    part = _dot(x_ref[...], w_ref[...])
    if nk == 1:
        o_ref[...] = part.astype(o_ref.dtype)
    else:
        k = pl.program_id(2)

        @pl.when(k == 0)
        def _():
            o_ref[...] = part

        @pl.when(k > 0)
        def _():
            o_ref[...] += part


def _matmul(x, w, *, bm, bn, bk, out_dtype, vmem_mib):
    m, kdim = x.shape
    n = w.shape[1]
    nk = kdim // bk
    assert m % bm == 0 and n % bn == 0 and kdim % bk == 0
    assert nk == 1 or out_dtype == F32
    return pl.pallas_call(
        functools.partial(_mm_kernel, nk=nk),
        grid=(m // bm, n // bn, nk),
        in_specs=[pl.BlockSpec((bm, bk), lambda i, j, k: (i, k)),
                  pl.BlockSpec((bk, bn), lambda i, j, k: (k, j))],
        out_specs=pl.BlockSpec((bm, bn), lambda i, j, k: (i, j)),
        out_shape=jax.ShapeDtypeStruct((m, n), out_dtype),
        compiler_params=_params(("parallel", "parallel", "arbitrary"), vmem_mib),
    )(x, w)


def _swiglu_kernel(x_ref, wg_ref, wu_ref, o_ref):
    x = x_ref[...]
    g = _dot(x, wg_ref[...])
    u = _dot(x, wu_ref[...])
    o_ref[...] = (g * jax.nn.sigmoid(g) * u).astype(o_ref.dtype)


def _swiglu_in(xb, w_gu, *, bm, bn):
    m, kdim = xb.shape
    f = w_gu.shape[1] // 2
    nj = f // bn
    return pl.pallas_call(
        _swiglu_kernel,
        grid=(m // bm, nj),
        in_specs=[pl.BlockSpec((bm, kdim), lambda i, j: (i, 0)),
                  pl.BlockSpec((kdim, bn), lambda i, j: (0, j)),
                  pl.BlockSpec((kdim, bn), lambda i, j: (0, j + nj))],
        out_specs=pl.BlockSpec((bm, bn), lambda i, j: (i, j)),
        out_shape=jax.ShapeDtypeStruct((m, f), BF16),
        compiler_params=_params(("parallel", "parallel"), 48),
    )(xb, w_gu, w_gu)


def _ln_kernel(x_ref, f_ref, g_ref, b_ref, o_ref, ob_ref, *, alpha, scale):
    y = alpha * x_ref[...] + scale * f_ref[...]
    mu = jnp.mean(y, axis=-1, keepdims=True)
    d = y - mu
    var = jnp.mean(d * d, axis=-1, keepdims=True)
    out = d * lax.rsqrt(var + LN_EPS) * g_ref[...] + b_ref[...]
    o_ref[...] = out
    ob_ref[...] = out.astype(BF16)


def _residual_ln(x, f, g, b, *, alpha, scale, bm=256):
    m, d = x.shape
    row = pl.BlockSpec((bm, d), lambda i: (i, 0))
    vec = pl.BlockSpec((1, d), lambda i: (0, 0))
    return pl.pallas_call(
        functools.partial(_ln_kernel, alpha=alpha, scale=scale),
        grid=(m // bm,),
        in_specs=[row, row, vec, vec],
        out_specs=[row, row],
        out_shape=[jax.ShapeDtypeStruct((m, d), F32), jax.ShapeDtypeStruct((m, d), BF16)],
        compiler_params=_params(("parallel",), 40),
    )(x, f, g.reshape(1, d), b.reshape(1, d))


def _s5_kernel(u_ref, h0_ref, min_ref, minsw_ref, toep_ref, mout_ref, dec_ref, y_ref, hfin_ref,
               s_sc, ssw_sc, hprev_sc, *, n_seq, n_chunk, n_single):
    u = u_ref[...]
    u_hi, u_lo = _split_bf16(u)

    def dot3_u(w):
        w_hi, w_lo = _split_bf16(w)
        return _dot(u_hi, w_hi) + (_dot(u_lo, w_hi) + _dot(u_hi, w_lo))

    s_sc[...] = dot3_u(min_ref[...])
    ssw_sc[...] = dot3_u(minsw_ref[...])
    a1 = dec_ref[0:1, :]
    a2 = dec_ref[1:2, :]
    a2sw = dec_ref[2:3, :]

    h = jnp.zeros((n_seq, s_sc.shape[1]), F32)
    hsw = h
    for k in range(n_chunk):
        rows = slice(k * n_seq, (k + 1) * n_seq)
        hprev_sc[rows, :] = h
        h, hsw = (a1 * h + a2 * hsw + s_sc[rows, :], a1 * hsw + a2sw * h + ssw_sc[rows, :])
    n_chain = n_chunk * n_seq
    hfin_ref[0:n_seq, :] = h
    h0 = h0_ref[...]
    h0sw = pltpu.roll(h0, h0.shape[1] // 2, axis=1)
    hprev_sc[n_chain:n_chain + n_single, :] = h0
    hfin_ref[n_seq:n_seq + n_single, :] = a1 * h0 + a2 * h0sw + s_sc[n_chain:n_chain + n_single, :]

    y_ref[...] = dot3_u(toep_ref[...]) + _dot3(hprev_sc[...], mout_ref[...])


def _s5_scan(u_g, h0_g, mats, *, n_seq, n_chunk, n_single):
    m_in, m_in_sw, toep, m_out, dec = mats
    g, rows, width = u_g.shape
    p2 = m_in.shape[2]
    n_out = n_seq + n_single
    grp = lambda *shape: pl.BlockSpec((None,) + shape, lambda i: (i,) + (0,) * len(shape))
    return pl.pallas_call(
        functools.partial(_s5_kernel, n_seq=n_seq, n_chunk=n_chunk, n_single=n_single),
        grid=(g,),
        in_specs=[grp(rows, width), grp(n_single, p2), grp(width, p2), grp(width, p2),
                  grp(width, width), grp(p2, width), grp(3, p2)],
        out_specs=[grp(rows, width), grp(n_out, p2)],
        out_shape=[jax.ShapeDtypeStruct((g, rows, width), F32),
                   jax.ShapeDtypeStruct((g, n_out, p2), F32)],
        scratch_shapes=[pltpu.VMEM((rows, p2), F32), pltpu.VMEM((rows, p2), F32),
                        pltpu.VMEM((rows, p2), F32)],
        compiler_params=_params(("parallel",), 40),
    )(u_g, h0_g, m_in, m_in_sw, toep, m_out, dec)


def _s5_matrices(lam_re, lam_im, log_step, b_re, b_im, c_re, c_im):
    hp = lax.Precision.HIGHEST
    L = S5_CHUNK
    g, p = lam_re.shape
    ch = b_re.shape[2]
    lam = lax.complex(lam_re.astype(F32), lam_im.astype(F32))
    step = jnp.exp(log_step.astype(F32))[:, None]
    lam_step = lam * step
    lam_bar = jnp.exp(lam_step)
    b_bar = ((lam_bar - 1.0) / lam)[..., None] * lax.complex(b_re.astype(F32), b_im.astype(F32))
    c_mat = lax.complex(c_re.astype(F32), c_im.astype(F32))
    d = jnp.arange(L + 1, dtype=F32)
    pw = jnp.exp(lam_step[None] * d[:, None, None])
    w_in = pw[:L][::-1].transpose(1, 0, 2)[:, :, None, :] * b_bar.transpose(0, 2, 1)[:, None, :, :]
    w_in = w_in.reshape(g, L * ch, p)
    m_in = jnp.concatenate([jnp.real(w_in), jnp.imag(w_in)], axis=-1)
    m_in_sw = jnp.concatenate([jnp.imag(w_in), jnp.real(w_in)], axis=-1)
    w_out = pw[1:].transpose(1, 2, 0)[:, :, :, None] * c_mat.transpose(0, 2, 1)[:, :, None, :]
    w_out = w_out.reshape(g, p, L * ch)
    m_out = jnp.concatenate([jnp.real(w_out), -jnp.imag(w_out)], axis=1)
    cp = c_mat[None] * pw[:L][:, :, None, :]
    kd = (jnp.einsum('dgcp,gpe->dgce', jnp.real(cp), jnp.real(b_bar), precision=hp)
          - jnp.einsum('dgcp,gpe->dgce', jnp.imag(cp), jnp.imag(b_bar), precision=hp))
    sig = jnp.arange(L)[:, None]
    tau = jnp.arange(L)[None, :]
    diff = tau - sig
    blocks = jnp.where((diff >= 0)[:, :, None, None, None], kd[jnp.clip(diff, 0, L - 1)], 0.0)
    toep = blocks.transpose(2, 0, 4, 1, 3).reshape(g, L * ch, L * ch)
    pl_ = pw[L]
    dec = jnp.stack([jnp.concatenate([jnp.real(pl_), jnp.real(pl_)], -1),
                     jnp.concatenate([-jnp.imag(pl_), jnp.imag(pl_)], -1),
                     jnp.concatenate([jnp.imag(pl_), -jnp.imag(pl_)], -1)], axis=1)
    return m_in, m_in_sw, toep, m_out, dec


def _glu_kernel(y_ref, u_ref, d_ref, w_ref, b_ref, o_ref):
    y = jax.nn.gelu(y_ref[...] + d_ref[...] * u_ref[...])
    z = _dot(y.astype(BF16), w_ref[...]) + b_ref[...]
    o_ref[...] = (y * jax.nn.sigmoid(z)).astype(BF16)


def _s5_glu(y_pre, proj, d_skip, w_glu, b_glu, *, bm=512):
    m, w = y_pre.shape
    row = pl.BlockSpec((bm, w), lambda i: (i, 0))
    vec = pl.BlockSpec((1, w), lambda i: (0, 0))
    return pl.pallas_call(
        _glu_kernel,
        grid=(m // bm,),
        in_specs=[row, row, vec, pl.BlockSpec((w, w), lambda i: (0, 0)), vec],
        out_specs=row,
        out_shape=jax.ShapeDtypeStruct((m, w), BF16),
        compiler_params=_params(("parallel",), 32),
    )(y_pre, proj, d_skip.reshape(1, w), w_glu, b_glu.reshape(1, w))


def _lru_kernel(x_ref, gate_ref, h0_ref, c0_ref, cw_ref, cb_ref, wa_ref, ba_ref, wx_ref, bx_ref, sp_ref,
                y_ref, hout_ref, cout_ref, xpad_sc, a_sc, b_sc, hs_sc, h_sc,
                *, rows, n_chain_blocks, blocks_per_seq, conv_width):
    blk = pl.program_id(0)
    tail = conv_width - 1
    base = 8
    is_start = jnp.logical_or(blk >= n_chain_blocks, blk % blocks_per_seq == 0)

    @pl.when(is_start)
    def _():
        h_sc[...] = h0_ref[...]
        xpad_sc[base - tail:base, :] = c0_ref[...]

    x = x_ref[...]
    xpad_sc[base:base + rows, :] = x
    xc = cb_ref[...]
    for j in range(conv_width):
        off = base - tail + j
        xc = xc + cw_ref[j:j + 1, :] * xpad_sc[off:off + rows, :]
    new_tail = x_ref[rows - tail:rows, :]
    xpad_sc[base - tail:base, :] = new_tail
    cout_ref[...] = new_tail

    xcb = xc.astype(BF16)
    r = jax.nn.sigmoid(_dot(xcb, wa_ref[...]) + ba_ref[...])
    ig = jax.nn.sigmoid(_dot(xcb, wx_ref[...]) + bx_ref[...])
    log_a = -LRU_C * r * sp_ref[...]
    a = jnp.exp(log_a)
    mult = jnp.sqrt(-jnp.tanh(log_a) * (a * a + 1.0))
    a_sc[...] = a
    b_sc[...] = mult * (ig * xc)

    def step(t, h):
        h = a_sc[pl.ds(t, 1), :] * h + b_sc[pl.ds(t, 1), :]
        hs_sc[pl.ds(t, 1), :] = h
        return h

    h = lax.fori_loop(0, rows, step, h_sc[...], unroll=8)
    h_sc[...] = h
    hout_ref[...] = h
    y_ref[...] = (hs_sc[...] * jax.nn.gelu(gate_ref[...])).astype(BF16)


def _rglru(proj, h0_all, c0_all, conv_w, conv_b, wa_bd, b_a, wx_bd, b_x, sp, *, m, width, x_col, gate_col,
           n_chain_seq, blocks_per_seq):
    rows = LRU_ROWS
    nblk = m // rows
    n_chain_blocks = n_chain_seq * blocks_per_seq
    n_seq = h0_all.shape[0]
    cw = conv_w.shape[0]

    def seq_of(i):
        return jnp.where(i < n_chain_blocks, i // blocks_per_seq, n_chain_seq + i - n_chain_blocks)

    vec = pl.BlockSpec((1, width), lambda i: (0, 0))
    mat = pl.BlockSpec((width, width), lambda i: (0, 0))
    hspec = pl.BlockSpec((None, 1, width), lambda i: (seq_of(i), 0, 0))
    cspec = pl.BlockSpec((None, cw - 1, width), lambda i: (seq_of(i), 0, 0))
    return pl.pallas_call(
        functools.partial(_lru_kernel, rows=rows, n_chain_blocks=n_chain_blocks,
                          blocks_per_seq=blocks_per_seq, conv_width=cw),
        grid=(nblk,),
        in_specs=[pl.BlockSpec((rows, width), lambda i: (i, x_col)),
                  pl.BlockSpec((rows, width), lambda i: (i, gate_col)),
                  hspec, cspec, pl.BlockSpec((cw, width), lambda i: (0, 0)), vec, mat, vec, mat, vec, vec],
        out_specs=[pl.BlockSpec((rows, width), lambda i: (i, 0)), hspec, cspec],
        out_shape=[jax.ShapeDtypeStruct((m, width), BF16),
                   jax.ShapeDtypeStruct((n_seq, 1, width), F32),
                   jax.ShapeDtypeStruct((n_seq, cw - 1, width), F32)],
        scratch_shapes=[pltpu.VMEM((rows + 8, width), F32), pltpu.VMEM((rows, width), F32),
                        pltpu.VMEM((rows, width), F32), pltpu.VMEM((rows, width), F32),
                        pltpu.VMEM((1, width), F32)],
        compiler_params=_params(("arbitrary",), 32),
    )(proj, proj, h0_all, c0_all, conv_w, conv_b.reshape(1, width), wa_bd, b_a.reshape(1, width),
      wx_bd, b_x.reshape(1, width), sp.reshape(1, width))


def _block_diag(w):
    n, c, d = w.shape
    eye = jnp.eye(n, dtype=w.dtype)
    return (eye[:, None, :, None] * w[:, :, None, :]).reshape(n * c, n * d)


def _head_norm(o, g, out_scale):
    o = o * lax.rsqrt(jnp.mean(o * o, axis=-1, keepdims=True) + LN_EPS) * g
    return o * out_scale


def _attn_prompt_kernel(scal_ref, q_ref, k_ref, v_ref, g_ref, o_ref, m_sc, l_sc, acc_sc,
                        *, bq, bk, hd, n_heads, out_scale):
    h = pl.program_id(1)
    i = pl.program_id(2)
    slope = scal_ref[h]
    lam = scal_ref[n_heads]
    scale = hd ** -0.5
    shift = int(math.log2(CHUNK))
    q = q_ref[...].astype(BF16)
    qpos = i * bq + lax.broadcasted_iota(jnp.int32, (bq, 1), 0)
    m_sc[...] = jnp.full(m_sc.shape, NEG_INF, F32)
    l_sc[...] = jnp.zeros(l_sc.shape, F32)
    acc_sc[...] = jnp.zeros(acc_sc.shape, F32)

    def body(j, carry):
        start = pl.multiple_of(j * bk, bk)
        kj = k_ref[pl.ds(start, bk), :].astype(BF16)
        vj = v_ref[pl.ds(start, bk), :].astype(BF16)
        kpos = j * bk + lax.broadcasted_iota(jnp.int32, (1, bk), 1)
        bias = slope * jnp.abs(qpos - kpos).astype(F32)
        visible = lax.shift_right_arithmetic(kpos, shift) <= lax.shift_right_arithmetic(qpos, shift)
        for c in range(2):
            s = _dot_nt(q[:, c * hd:(c + 1) * hd], kj[:, c * hd:(c + 1) * hd]) * scale - bias
            s = jnp.where(visible, s, NEG_INF)
            m_old = m_sc[c]
            m_new = jnp.maximum(m_old, jnp.max(s, axis=-1, keepdims=True))
            p = jnp.exp(s - m_new)
            corr = jnp.exp(m_old - m_new)
            l_sc[c] = corr * l_sc[c] + jnp.sum(p, axis=-1, keepdims=True)
            acc_sc[c] = corr * acc_sc[c] + _dot(p.astype(BF16), vj)
            m_sc[c] = m_new
        return carry

    n_kv = (i * bq + bq + bk - 1) // bk
    lax.fori_loop(0, n_kv, body, 0)
    o = acc_sc[0] / l_sc[0] - lam * (acc_sc[1] / l_sc[1])
    o_ref[...] = _head_norm(o, g_ref[...], out_scale).astype(BF16)


def _attn_prompt(proj, scal, subln, *, n_batch, seq, n_heads, hd, q_col, k_col, v_col, out_rows, out_scale,
                 bq=256, bk=256):
    assert bq % CHUNK == 0 and bk % CHUNK == 0 and seq % bq == 0 and seq % bk == 0
    nq = seq // bq
    e = 2 * hd
    return pl.pallas_call(
        functools.partial(_attn_prompt_kernel, bq=bq, bk=bk, hd=hd, n_heads=n_heads, out_scale=out_scale),
        grid=(n_batch, n_heads, nq),
        in_specs=[pl.BlockSpec(memory_space=pltpu.SMEM),
                  pl.BlockSpec((bq, e), lambda b, h, i: (b * nq + i, q_col + h)),
                  pl.BlockSpec((seq, e), lambda b, h, i: (b, k_col + h)),
                  pl.BlockSpec((seq, e), lambda b, h, i: (b, v_col + h)),
                  pl.BlockSpec((1, e), lambda b, h, i: (0, 0))],
        out_specs=pl.BlockSpec((bq, e), lambda b, h, i: (b * nq + i, h)),
        out_shape=jax.ShapeDtypeStruct((out_rows, n_heads * e), BF16),
        scratch_shapes=[pltpu.VMEM((2, bq, 1), F32), pltpu.VMEM((2, bq, 1), F32),
                        pltpu.VMEM((2, bq, e), F32)],
        compiler_params=_params(("parallel", "parallel", "arbitrary"), 40),
    )(scal, proj, proj, proj, subln.reshape(1, e))


def _attn_sample_kernel(scal_ref, q_ref, kn_ref, vn_ref, kc_ref, vc_ref, g_ref, o_ref,
                        *, hd, n_heads, past, out_scale):
    h = pl.program_id(1)
    slope = scal_ref[h]
    lam = scal_ref[n_heads]
    scale = hd ** -0.5
    shift = int(math.log2(CHUNK))
    t = q_ref.shape[0]
    q = q_ref[...].astype(BF16)
    kc = kc_ref[...].astype(BF16)
    vc = vc_ref[...].astype(BF16)
    kn = kn_ref[...].astype(BF16)
    vn = vn_ref[...].astype(BF16)
    qpos = past + lax.broadcasted_iota(jnp.int32, (t, 1), 0)
    qchunk = lax.shift_right_arithmetic(qpos, shift)

    def masked_bias(kpos):
        bias = slope * jnp.abs(qpos - kpos).astype(F32)
        return bias, lax.shift_right_arithmetic(kpos, shift) <= qchunk

    bias_c, vis_c = masked_bias(lax.broadcasted_iota(jnp.int32, (1, past), 1))
    bias_n, vis_n = masked_bias(past + lax.broadcasted_iota(jnp.int32, (1, t), 1))
    outs = []
    for c in range(2):
        cols = slice(c * hd, (c + 1) * hd)
        s_c = jnp.where(vis_c, _dot_nt(q[:, cols], kc[:, cols]) * scale - bias_c, NEG_INF)
        s_n = jnp.where(vis_n, _dot_nt(q[:, cols], kn[:, cols]) * scale - bias_n, NEG_INF)
        m = jnp.maximum(jnp.max(s_c, axis=-1, keepdims=True), jnp.max(s_n, axis=-1, keepdims=True))
        p_c = jnp.exp(s_c - m)
        p_n = jnp.exp(s_n - m)
        l = jnp.sum(p_c, axis=-1, keepdims=True) + jnp.sum(p_n, axis=-1, keepdims=True)
        outs.append((_dot(p_c.astype(BF16), vc) + _dot(p_n.astype(BF16), vn)) / l)
    o = outs[0] - lam * outs[1]
    o_ref[...] = _head_norm(o, g_ref[...], out_scale).astype(BF16)


def _attn_sample(proj, cache_k, cache_v, scal, subln, *, layer, n_batch, seq, n_heads, hd, q_col, k_col, v_col,
                 row0, out_scale):
    e = 2 * hd
    past = cache_k.shape[1]
    rb0 = row0 // seq
    new = lambda col: pl.BlockSpec((seq, e), lambda b, h: (rb0 + b, col + h))
    old = pl.BlockSpec((None, past, e), lambda b, h: (layer * n_batch + b, 0, h))
    return pl.pallas_call(
        functools.partial(_attn_sample_kernel, hd=hd, n_heads=n_heads, past=past, out_scale=out_scale),
        grid=(n_batch, n_heads),
        in_specs=[pl.BlockSpec(memory_space=pltpu.SMEM), new(q_col), new(k_col), new(v_col), old, old,
                  pl.BlockSpec((1, e), lambda b, h: (0, 0))],
        out_specs=pl.BlockSpec((seq, e), lambda b, h: (b, h)),
        out_shape=jax.ShapeDtypeStruct((n_batch * seq, n_heads * e), BF16),
        compiler_params=_params(("parallel", "parallel"), 40),
    )(scal, proj, proj, proj, cache_k, cache_v, subln.reshape(1, e))


def _merge_kernel(ya_ref, yb_ref, yc_ref, wa_ref, wb_ref, wc_ref, ga_ref, gb_ref, gc_ref, bg_ref, o_ref):
    def branch(y_ref, w_ref, g_ref, r):
        return jax.nn.sigmoid(g_ref[...] + bg_ref[r:r + 1, :]) * _dot(y_ref[...], w_ref[...])

    merged = branch(ya_ref, wa_ref, ga_ref, 0) + branch(yb_ref, wb_ref, gb_ref, 1) + branch(yc_ref, wc_ref, gc_ref, 2)
    o_ref[...] = merged.astype(BF16)


def _merge(ya, yb, yc, wa, wb, wc, proj, b_gate, *, gate_col, bm=512, bn=1024):
    m = ya.shape[0]
    d = wa.shape[1]
    nj = d // bn
    y_spec = lambda y: pl.BlockSpec((bm, y.shape[1]), lambda i, j: (i, 0))
    w_spec = lambda w: pl.BlockSpec((w.shape[0], bn), lambda i, j: (0, j))
    g_spec = lambda r: pl.BlockSpec((bm, bn), lambda i, j: (i, gate_col + r * nj + j))
    return pl.pallas_call(
        _merge_kernel,
        grid=(m // bm, nj),
        in_specs=[y_spec(ya), y_spec(yb), y_spec(yc), w_spec(wa), w_spec(wb), w_spec(wc),
                  g_spec(0), g_spec(1), g_spec(2), pl.BlockSpec((3, bn), lambda i, j: (0, j))],
        out_specs=pl.BlockSpec((bm, bn), lambda i, j: (i, j)),
        out_shape=jax.ShapeDtypeStruct((m, d), BF16),
        compiler_params=_params(("parallel", "parallel"), 48),
    )(ya, yb, yc, wa, wb, wc, proj, proj, proj, b_gate)


def _ple_kernel(xb_ref, wg_ref, p_ref, wp_ref, x_ref, o_ref, ob_ref):
    gate = jax.nn.sigmoid(_dot(xb_ref[...], wg_ref[...]))
    out = x_ref[...] + gate * _dot(p_ref[...], wp_ref[...])
    o_ref[...] = out
    ob_ref[...] = out.astype(BF16)


def _ple(x, xb, pb, w_gate, w_proj, *, bm=512, bn=1024):
    m, d = x.shape
    pdim = pb.shape[1]
    tile = pl.BlockSpec((bm, bn), lambda i, j: (i, j))
    return pl.pallas_call(
        _ple_kernel,
        grid=(m // bm, d // bn),
        in_specs=[pl.BlockSpec((bm, d), lambda i, j: (i, 0)), pl.BlockSpec((d, bn), lambda i, j: (0, j)),
                  pl.BlockSpec((bm, pdim), lambda i, j: (i, 0)), pl.BlockSpec((pdim, bn), lambda i, j: (0, j)),
                  tile],
        out_specs=[tile, tile],
        out_shape=[jax.ShapeDtypeStruct((m, d), F32), jax.ShapeDtypeStruct((m, d), BF16)],
        compiler_params=_params(("parallel", "parallel"), 48),
    )(xb, w_gate, pb, w_proj, x)


def _pad_ffn(w_in, w_out, f_pad):
    d, f2 = w_in.shape
    f = f2 // 2
    pad = f_pad - f
    gate = jnp.pad(w_in[:, :f].astype(BF16), ((0, 0), (0, pad)))
    up = jnp.pad(w_in[:, f:].astype(BF16), ((0, 0), (0, pad)))
    return jnp.concatenate([gate, up], axis=1), jnp.pad(w_out.astype(BF16), ((0, pad), (0, 0)))


def kernel(x_prompt, x_sample, cache_k, cache_v, state_s5_re, state_s5_im, state_lru, state_conv, p_prompt, p_sample, ln_g, ln_b, ffn_w_in, ffn_w_out, w_in, b_gate, s5_lam_re, s5_lam_im, s5_log_step, s5_b_re, s5_b_im, s5_c_re, s5_c_im, s5_d, s5_w_glu, s5_b_glu, diff_lambda, diff_subln, lru_conv_w, lru_conv_b, lru_w_a, lru_b_a, lru_w_x, lru_b_x, lru_lambda, w_br_a, w_br_b, w_br_c, w_o, ple_w_proj, ple_w_gate):
    bp, tp, d_model = x_prompt.shape
    bs, ts, _ = x_sample.shape
    depth = ln_g.shape[0]
    mp, ms = bp * tp, bs * ts
    m = mp + ms
    n_heads, e = cache_k.shape[3], cache_k.shape[4]
    hd = e // 2
    past = cache_k.shape[2]
    groups, p_state, grp_ch = s5_b_re.shape[1:]
    s5_w = groups * grp_ch
    att_w = n_heads * e
    lru_w = lru_lambda.shape[1]
    d_ff = ffn_w_out.shape[2]
    f_pad = -(-d_ff // 1024) * 1024
    alpha = (2 * depth) ** 0.25
    L = S5_CHUNK
    assert ts == L and tp % L == 0 and ts == LRU_ROWS and ts == CHUNK and past % CHUNK == 0
    kp = tp // L
    c_q, c_k, c_v = s5_w, s5_w + att_w, s5_w + 2 * att_w
    c_xr = s5_w + 3 * att_w
    c_gr = c_xr + lru_w
    c_gl = c_gr + lru_w

    x = jnp.concatenate([x_prompt.reshape(mp, d_model), x_sample.reshape(ms, d_model)], axis=0)
    xb = x.astype(BF16)
    cache_k2 = cache_k.reshape(depth * bs, past, att_w)
    cache_v2 = cache_v.reshape(depth * bs, past, att_w)
    slopes = jnp.exp2(-8.0 * (jnp.arange(n_heads, dtype=F32) + 1.0) / n_heads)

    def ffn(i, s, x, xb):
        w_gu, w_dn = _pad_ffn(ffn_w_in[i, s], ffn_w_out[i, s], f_pad)
        hdn = _swiglu_in(xb, w_gu, bm=1024, bn=512)
        f = _matmul(hdn, w_dn, bm=1024, bn=1024, bk=f_pad // 4, out_dtype=F32, vmem_mib=48)
        return _residual_ln(x, f, ln_g[i, 2 * s], ln_b[i, 2 * s], alpha=alpha, scale=0.5)

    ks, vs, s5r, s5i, lruh, convs = [], [], [], [], [], []
    for i in range(depth):
        x, xb = ffn(i, 0, x, xb)

        proj = _matmul(xb, w_in[i].astype(BF16), bm=1024, bn=1024, bk=d_model, out_dtype=F32, vmem_mib=48)

        u_p = proj[:mp, :s5_w].reshape(bp, kp, L, groups, grp_ch).transpose(3, 1, 0, 2, 4).reshape(groups, kp * bp, L * grp_ch)
        u_s = proj[mp:, :s5_w].reshape(bs, L, groups, grp_ch).transpose(2, 0, 1, 3).reshape(groups, bs, L * grp_ch)
        u_g = jnp.concatenate([u_p, u_s], axis=1)
        h0_g = jnp.concatenate([state_s5_re[i], state_s5_im[i]], axis=-1).transpose(1, 0, 2)
        mats = _s5_matrices(s5_lam_re[i], s5_lam_im[i], s5_log_step[i], s5_b_re[i], s5_b_im[i], s5_c_re[i], s5_c_im[i])
        y_g, hfin = _s5_scan(u_g, h0_g, mats, n_seq=bp, n_chunk=kp, n_single=bs)
        y_p = y_g[:, :kp * bp].reshape(groups, kp, bp, L, grp_ch).transpose(2, 1, 3, 0, 4).reshape(mp, s5_w)
        y_s = y_g[:, kp * bp:].reshape(groups, bs, L, grp_ch).transpose(1, 2, 0, 3).reshape(ms, s5_w)
        y_a = _s5_glu(jnp.concatenate([y_p, y_s], axis=0), proj, s5_d[i], s5_w_glu[i].astype(BF16), s5_b_glu[i])
        hfin = hfin.transpose(1, 0, 2)
        s5r.append((hfin[:bp, :, :p_state], hfin[bp:, :, :p_state]))
        s5i.append((hfin[:bp, :, p_state:], hfin[bp:, :, p_state:]))

        lam_init = 0.8 - 0.6 * math.exp(-0.3 * i)
        dl = diff_lambda[i].astype(F32)
        lam = jnp.exp(jnp.sum(dl[0] * dl[1])) - jnp.exp(jnp.sum(dl[2] * dl[3])) + lam_init
        scal = jnp.concatenate([slopes, lam.reshape(1)])
        acols = dict(q_col=c_q // e, k_col=c_k // e, v_col=c_v // e)
        yb_p = _attn_prompt(proj, scal, diff_subln[i], n_batch=bp, seq=tp, n_heads=n_heads, hd=hd,
                            out_rows=mp, out_scale=1.0 - lam_init, **acols)
        yb_s = _attn_sample(proj, cache_k2, cache_v2, scal, diff_subln[i], layer=i, n_batch=bs, seq=ts,
                            n_heads=n_heads, hd=hd, row0=mp, out_scale=1.0 - lam_init, **acols)
        y_b = jnp.concatenate([yb_p, yb_s], axis=0)

        h0_all = jnp.concatenate([jnp.zeros((bp, lru_w), F32), state_lru[i]], axis=0).reshape(bp + bs, 1, lru_w)
        c0_all = jnp.concatenate([jnp.zeros((bp,) + state_conv.shape[2:], F32), state_conv[i]], axis=0)
        sp = jax.nn.softplus(-lru_lambda[i].astype(F32))
        y_c, h_all, c_all = _rglru(proj, h0_all, c0_all, lru_conv_w[i], lru_conv_b[i],
                                   _block_diag(lru_w_a[i]).astype(BF16), lru_b_a[i].reshape(-1),
                                   _block_diag(lru_w_x[i]).astype(BF16), lru_b_x[i].reshape(-1), sp,
                                   m=m, width=lru_w, x_col=c_xr // lru_w, gate_col=c_gr // lru_w,
                                   n_chain_seq=bp, blocks_per_seq=tp // LRU_ROWS)
        lruh.append((h_all[:bp, 0], h_all[bp:, 0]))
        convs.append((c_all[:bp], c_all[bp:]))

        merged = _merge(y_a, y_b, y_c, w_br_a[i].astype(BF16), w_br_b[i].astype(BF16), w_br_c[i].astype(BF16),
                        proj, b_gate[i], gate_col=c_gl // 1024)
        mix = _matmul(merged, w_o[i].astype(BF16), bm=1024, bn=1024, bk=d_model, out_dtype=F32, vmem_mib=48)
        x, xb = _residual_ln(x, mix, ln_g[i, 1], ln_b[i, 1], alpha=alpha, scale=1.0)

        x, xb = ffn(i, 1, x, xb)

        pb = jnp.concatenate([p_prompt[i].reshape(mp, -1), p_sample[i].reshape(ms, -1)], axis=0).astype(BF16)
        x, xb = _ple(x, xb, pb, ple_w_gate[i].astype(BF16), ple_w_proj[i].astype(BF16))

        ks.append((proj[:mp, c_k:c_v].reshape(bp, tp, n_heads, e), proj[mp:, c_k:c_v].reshape(bs, ts, n_heads, e)))
        vs.append((proj[:mp, c_v:c_xr].reshape(bp, tp, n_heads, e), proj[mp:, c_v:c_xr].reshape(bs, ts, n_heads, e)))

    stack = lambda pairs, which: jnp.stack([pr[which] for pr in pairs])
    return (x[:mp].reshape(bp, tp, d_model), x[mp:].reshape(bs, ts, d_model),
            stack(ks, 0), stack(vs, 0), stack(s5r, 0), stack(s5i, 0), stack(lruh, 0), stack(convs, 0),
            stack(ks, 1), stack(vs, 1), stack(s5r, 1), stack(s5i, 1), stack(lruh, 1), stack(convs, 1))
```

```python
import functools
import math

import jax
import jax.numpy as jnp
from jax import lax
from jax.experimental import pallas as pl
from jax.experimental.pallas import tpu as pltpu

F32 = jnp.float32
BF16 = jnp.bfloat16

CHUNK = 64
LRU_C = 8.0
LN_EPS = 1e-5
NEG_INF = -1e30
S5_CHUNK = 64
LRU_ROWS = 64

LANES = 128
LOG2E = math.log2(math.e)

MIB = 1024 * 1024


def _params(semantics, vmem_mib):
    return pltpu.CompilerParams(dimension_semantics=semantics, vmem_limit_bytes=vmem_mib * MIB)


def _dot(a, b):
    return jnp.dot(a, b, preferred_element_type=F32)


def _dot_nt(a, b):
    return lax.dot_general(a, b, (((1,), (1,)), ((), ())), preferred_element_type=F32)


def _split_bf16(a):
    hi = a.astype(BF16)
    lo = (a - hi.astype(F32)).astype(BF16)
    return hi, lo


def _dot3(a, b):
    a_hi, a_lo = _split_bf16(a)
    b_hi, b_lo = _split_bf16(b)
    return _dot(a_hi, b_hi) + (_dot(a_lo, b_hi) + _dot(a_hi, b_lo))


def _mm_kernel(x_ref, w_ref, o_ref, *, nk):
    part = _dot(x_ref[...], w_ref[...])
    if nk == 1:
        o_ref[...] = part.astype(o_ref.dtype)
    else:
        k = pl.program_id(2)

        @pl.when(k == 0)
        def _():
            o_ref[...] = part

        @pl.when(k > 0)
        def _():
            o_ref[...] += part


def _matmul(x, w, *, w_lead=(), bm, bn, bk, out_dtype, vmem_mib):
    m, kdim = x.shape
    n = w.shape[-1]
    nk = kdim // bk
    assert w.shape[-2] == kdim and m % bm == 0 and n % bn == 0 and kdim % bk == 0
    assert nk == 1 or out_dtype == F32
    return pl.pallas_call(
        functools.partial(_mm_kernel, nk=nk),
        grid=(m // bm, n // bn, nk),
        in_specs=[pl.BlockSpec((bm, bk), lambda i, j, k: (i, k)),
                  pl.BlockSpec((None,) * len(w_lead) + (bk, bn), lambda i, j, k: w_lead + (k, j))],
        out_specs=pl.BlockSpec((bm, bn), lambda i, j, k: (i, j)),
        out_shape=jax.ShapeDtypeStruct((m, n), out_dtype),
        compiler_params=_params(("parallel", "parallel", "arbitrary"), vmem_mib),
    )(x, w)


def _mm_f32w_kernel(x_ref, w_ref, o_ref, w_sc):
    @pl.when(pl.program_id(1) == 0)
    def _():
        w_sc[...] = w_ref[...].astype(BF16)

    o_ref[...] = _dot(x_ref[...], w_sc[...]).astype(o_ref.dtype)


def _matmul_f32w(x, w, *, w_lead=(), bm, bn, out_dtype, vmem_mib):
    m, kdim = x.shape
    n = w.shape[-1]
    assert w.shape[-2] == kdim and m % bm == 0 and n % bn == 0
    return pl.pallas_call(
        _mm_f32w_kernel,
        grid=(n // bn, m // bm),
        in_specs=[pl.BlockSpec((bm, kdim), lambda j, i: (i, 0)),
                  pl.BlockSpec((None,) * len(w_lead) + (kdim, bn), lambda j, i: w_lead + (0, j))],
        out_specs=pl.BlockSpec((bm, bn), lambda j, i: (i, j)),
        out_shape=jax.ShapeDtypeStruct((m, n), out_dtype),
        scratch_shapes=[pltpu.VMEM((kdim, bn), BF16)],
        compiler_params=_params(("parallel", "arbitrary"), vmem_mib),
    )(x, w)


def _swiglu_kernel(x_ref, wg_ref, wu_ref, o_ref, wg_sc, wu_sc, *, n_real):
    j = pl.program_id(0)

    @pl.when(jnp.logical_and(pl.program_id(1) == 0, j < n_real))
    def _():
        wg_sc[...] = wg_ref[...].astype(BF16)
        wu_sc[...] = wu_ref[...].astype(BF16)

    @pl.when(j < n_real)
    def _():
        x = x_ref[...]
        g = _dot(x, wg_sc[...])
        u = _dot(x, wu_sc[...])
        o_ref[...] = (g * jax.nn.sigmoid(g) * u).astype(o_ref.dtype)

    @pl.when(j >= n_real)
    def _():
        o_ref[...] = jnp.zeros(o_ref.shape, o_ref.dtype)


def _swiglu_in(xb, w_gu, *, w_lead, f_pad, bm, bn):
    m, kdim = xb.shape
    f = w_gu.shape[-1] // 2
    n_real = f // bn
    nj = f_pad // bn
    assert f % bn == 0 and f_pad % bn == 0 and m % bm == 0
    lead = (None,) * len(w_lead)
    col = lambda j: jnp.minimum(j, n_real - 1)
    return pl.pallas_call(
        functools.partial(_swiglu_kernel, n_real=n_real),
        grid=(nj, m // bm),
        in_specs=[pl.BlockSpec((bm, kdim), lambda j, i: (i, 0)),
                  pl.BlockSpec(lead + (kdim, bn), lambda j, i: w_lead + (0, col(j))),
                  pl.BlockSpec(lead + (kdim, bn), lambda j, i: w_lead + (0, col(j) + n_real))],
        out_specs=pl.BlockSpec((bm, bn), lambda j, i: (i, j)),
        out_shape=jax.ShapeDtypeStruct((m, f_pad), BF16),
        scratch_shapes=[pltpu.VMEM((kdim, bn), BF16), pltpu.VMEM((kdim, bn), BF16)],
        compiler_params=_params(("parallel", "arbitrary"), 48),
    )(xb, w_gu, w_gu)


def _ln_kernel(x_ref, f_ref, g_ref, b_ref, o_ref, ob_ref, *, alpha, scale):
    y = alpha * x_ref[...] + scale * f_ref[...]
    mu = jnp.mean(y, axis=-1, keepdims=True)
    d = y - mu
    var = jnp.mean(d * d, axis=-1, keepdims=True)
    out = d * lax.rsqrt(var + LN_EPS) * g_ref[...] + b_ref[...]
    o_ref[...] = out
    ob_ref[...] = out.astype(BF16)


def _residual_ln(x, f, g, b, *, alpha, scale, bm=256):
    m, d = x.shape
    row = pl.BlockSpec((bm, d), lambda i: (i, 0))
    vec = pl.BlockSpec((1, d), lambda i: (0, 0))
    return pl.pallas_call(
        functools.partial(_ln_kernel, alpha=alpha, scale=scale),
        grid=(m // bm,),
        in_specs=[row, row, vec, vec],
        out_specs=[row, row],
        out_shape=[jax.ShapeDtypeStruct((m, d), F32), jax.ShapeDtypeStruct((m, d), BF16)],
        compiler_params=_params(("parallel",), 40),
    )(x, f, g.reshape(1, d), b.reshape(1, d))


def _s5_kernel(u_ref, h0_ref, min_ref, minsw_ref, toep_ref, mout_ref, dec_ref, y_ref, hfin_ref,
               s_sc, ssw_sc, hprev_sc, *, n_seq, n_chunk, n_single):
    u = u_ref[...]
    u_hi, u_lo = _split_bf16(u)

    def dot3_u(w):
        w_hi, w_lo = _split_bf16(w)
        return _dot(u_hi, w_hi) + (_dot(u_lo, w_hi) + _dot(u_hi, w_lo))

    s_sc[...] = dot3_u(min_ref[...])
    ssw_sc[...] = dot3_u(minsw_ref[...])
    a1 = dec_ref[0:1, :]
    a2 = dec_ref[1:2, :]
    a2sw = dec_ref[2:3, :]

    h = jnp.zeros((n_seq, s_sc.shape[1]), F32)
    hsw = h
    for k in range(n_chunk):
        rows = slice(k * n_seq, (k + 1) * n_seq)
        hprev_sc[rows, :] = h
        h, hsw = (a1 * h + a2 * hsw + s_sc[rows, :], a1 * hsw + a2sw * h + ssw_sc[rows, :])
    n_chain = n_chunk * n_seq
    hfin_ref[0:n_seq, :] = h
    h0 = h0_ref[...]
    h0sw = pltpu.roll(h0, h0.shape[1] // 2, axis=1)
    hprev_sc[n_chain:n_chain + n_single, :] = h0
    hfin_ref[n_seq:n_seq + n_single, :] = a1 * h0 + a2 * h0sw + s_sc[n_chain:n_chain + n_single, :]

    y_ref[...] = dot3_u(toep_ref[...]) + _dot3(hprev_sc[...], mout_ref[...])


def _s5_scan(u_g, h0_g, mats, *, n_seq, n_chunk, n_single):
    m_in, m_in_sw, toep, m_out, dec = mats
    g, rows, width = u_g.shape
    p2 = m_in.shape[2]
    n_out = n_seq + n_single
    grp = lambda *shape: pl.BlockSpec((None,) + shape, lambda i: (i,) + (0,) * len(shape))
    return pl.pallas_call(
        functools.partial(_s5_kernel, n_seq=n_seq, n_chunk=n_chunk, n_single=n_single),
        grid=(g,),
        in_specs=[grp(rows, width), grp(n_single, p2), grp(width, p2), grp(width, p2),
                  grp(width, width), grp(p2, width), grp(3, p2)],
        out_specs=[grp(rows, width), grp(n_out, p2)],
        out_shape=[jax.ShapeDtypeStruct((g, rows, width), F32),
                   jax.ShapeDtypeStruct((g, n_out, p2), F32)],
        scratch_shapes=[pltpu.VMEM((rows, p2), F32), pltpu.VMEM((rows, p2), F32),
                        pltpu.VMEM((rows, p2), F32)],
        compiler_params=_params(("parallel",), 40),
    )(u_g, h0_g, m_in, m_in_sw, toep, m_out, dec)


def _s5_matrices(lam_re, lam_im, log_step, b_re, b_im, c_re, c_im):
    hp = lax.Precision.HIGHEST
    L = S5_CHUNK
    g, p = lam_re.shape
    ch = b_re.shape[2]
    lam = lax.complex(lam_re.astype(F32), lam_im.astype(F32))
    step = jnp.exp(log_step.astype(F32))[:, None]
    lam_step = lam * step
    lam_bar = jnp.exp(lam_step)
    b_bar = ((lam_bar - 1.0) / lam)[..., None] * lax.complex(b_re.astype(F32), b_im.astype(F32))
    c_mat = lax.complex(c_re.astype(F32), c_im.astype(F32))
    d = jnp.arange(L + 1, dtype=F32)
    pw = jnp.exp(lam_step[None] * d[:, None, None])
    w_in = b_bar.transpose(0, 2, 1)[:, :, None, :] * pw[:L][::-1].transpose(1, 0, 2)[:, None, :, :]
    w_in = w_in.reshape(g, ch * L, p)
    m_in = jnp.concatenate([jnp.real(w_in), jnp.imag(w_in)], axis=-1)
    m_in_sw = jnp.concatenate([jnp.imag(w_in), jnp.real(w_in)], axis=-1)
    w_out = pw[1:].transpose(1, 2, 0)[:, :, :, None] * c_mat.transpose(0, 2, 1)[:, :, None, :]
    w_out = w_out.reshape(g, p, L * ch)
    m_out = jnp.concatenate([jnp.real(w_out), -jnp.imag(w_out)], axis=1)
    cp = c_mat[None] * pw[:L][:, :, None, :]
    kd = (jnp.einsum('dgcp,gpe->dgce', jnp.real(cp), jnp.real(b_bar), precision=hp)
          - jnp.einsum('dgcp,gpe->dgce', jnp.imag(cp), jnp.imag(b_bar), precision=hp))
    seq = jnp.pad(kd.transpose(1, 3, 0, 2), ((0, 0), (0, 0), (L, 1), (0, 0))).reshape(g, ch, (2 * L + 1) * ch)
    toep = jnp.tile(seq, (1, 1, L))[:, :, :L * 2 * L * ch].reshape(g, ch, L, 2 * L * ch)[..., L * ch:]
    toep = toep.reshape(g, ch * L, L * ch)
    pl_ = pw[L]
    dec = jnp.stack([jnp.concatenate([jnp.real(pl_), jnp.real(pl_)], -1),
                     jnp.concatenate([-jnp.imag(pl_), jnp.imag(pl_)], -1),
                     jnp.concatenate([jnp.imag(pl_), -jnp.imag(pl_)], -1)], axis=1)
    return m_in, m_in_sw, toep, m_out, dec


def _glu_kernel(y_ref, u_ref, d_ref, w_ref, b_ref, o_ref):
    y = jax.nn.gelu(y_ref[...] + d_ref[...] * u_ref[...])
    z = _dot(y.astype(BF16), w_ref[...]) + b_ref[...]
    o_ref[...] = (y * jax.nn.sigmoid(z)).astype(BF16)


def _s5_glu(y_pre, proj, d_skip, w_glu, b_glu, *, bm=512):
    m, w = y_pre.shape
    row = pl.BlockSpec((bm, w), lambda i: (i, 0))
    vec = pl.BlockSpec((1, w), lambda i: (0, 0))
    return pl.pallas_call(
        _glu_kernel,
        grid=(m // bm,),
        in_specs=[row, row, vec, pl.BlockSpec((w, w), lambda i: (0, 0)), vec],
        out_specs=row,
        out_shape=jax.ShapeDtypeStruct((m, w), BF16),
        compiler_params=_params(("parallel",), 32),
    )(y_pre, proj, d_skip.reshape(1, w), w_glu, b_glu.reshape(1, w))


def _lru_kernel(x_ref, gate_ref, h0_ref, c0_ref, cw_ref, cb_ref, wa_ref, ba_ref, wx_ref, bx_ref, sp_ref,
                y_ref, hout_ref, cout_ref, xpad_sc, a_sc, b_sc, hs_sc, h_sc,
                *, rows, n_chain_blocks, blocks_per_seq, conv_width):
    blk = pl.program_id(0)
    tail = conv_width - 1
    base = 8
    is_start = jnp.logical_or(blk >= n_chain_blocks, blk % blocks_per_seq == 0)

    @pl.when(is_start)
    def _():
        h_sc[...] = h0_ref[...]
        xpad_sc[base - tail:base, :] = c0_ref[...]

    x = x_ref[...]
    xpad_sc[base:base + rows, :] = x
    xc = cb_ref[...]
    for j in range(conv_width):
        off = base - tail + j
        xc = xc + cw_ref[j:j + 1, :] * xpad_sc[off:off + rows, :]
    new_tail = x_ref[rows - tail:rows, :]
    xpad_sc[base - tail:base, :] = new_tail
    cout_ref[...] = new_tail

    xcb = xc.astype(BF16)
    r = jax.nn.sigmoid(_dot(xcb, wa_ref[...]) + ba_ref[...])
    ig = jax.nn.sigmoid(_dot(xcb, wx_ref[...]) + bx_ref[...])
    log_a = -LRU_C * r * sp_ref[...]
    a = jnp.exp(log_a)
    mult = jnp.sqrt(-jnp.tanh(log_a) * (a * a + 1.0))
    a_sc[...] = a
    b_sc[...] = mult * (ig * xc)

    def step(t, h):
        h = a_sc[pl.ds(t, 1), :] * h + b_sc[pl.ds(t, 1), :]
        hs_sc[pl.ds(t, 1), :] = h
        return h

    h = lax.fori_loop(0, rows, step, h_sc[...], unroll=8)
    h_sc[...] = h
    hout_ref[...] = h
    y_ref[...] = (hs_sc[...] * jax.nn.gelu(gate_ref[...])).astype(BF16)


def _rglru(proj, h0_all, c0_all, conv_w, conv_b, wa_bd, b_a, wx_bd, b_x, sp, *, m, width, x_col, gate_col,
           n_chain_seq, blocks_per_seq):
    rows = LRU_ROWS
    nblk = m // rows
    n_chain_blocks = n_chain_seq * blocks_per_seq
    n_seq = h0_all.shape[0]
    cw = conv_w.shape[0]

    def seq_of(i):
        return jnp.where(i < n_chain_blocks, i // blocks_per_seq, n_chain_seq + i - n_chain_blocks)

    vec = pl.BlockSpec((1, width), lambda i: (0, 0))
    mat = pl.BlockSpec((width, width), lambda i: (0, 0))
    hspec = pl.BlockSpec((None, 1, width), lambda i: (seq_of(i), 0, 0))
    cspec = pl.BlockSpec((None, cw - 1, width), lambda i: (seq_of(i), 0, 0))
    return pl.pallas_call(
        functools.partial(_lru_kernel, rows=rows, n_chain_blocks=n_chain_blocks,
                          blocks_per_seq=blocks_per_seq, conv_width=cw),
        grid=(nblk,),
        in_specs=[pl.BlockSpec((rows, width), lambda i: (i, x_col)),
                  pl.BlockSpec((rows, width), lambda i: (i, gate_col)),
                  hspec, cspec, pl.BlockSpec((cw, width), lambda i: (0, 0)), vec, mat, vec, mat, vec, vec],
        out_specs=[pl.BlockSpec((rows, width), lambda i: (i, 0)), hspec, cspec],
        out_shape=[jax.ShapeDtypeStruct((m, width), BF16),
                   jax.ShapeDtypeStruct((n_seq, 1, width), F32),
                   jax.ShapeDtypeStruct((n_seq, cw - 1, width), F32)],
        scratch_shapes=[pltpu.VMEM((rows + 8, width), F32), pltpu.VMEM((rows, width), F32),
                        pltpu.VMEM((rows, width), F32), pltpu.VMEM((rows, width), F32),
                        pltpu.VMEM((1, width), F32)],
        compiler_params=_params(("arbitrary",), 32),
    )(proj, proj, h0_all, c0_all, conv_w, conv_b.reshape(1, width), wa_bd, b_a.reshape(1, width),
      wx_bd, b_x.reshape(1, width), sp.reshape(1, width))


def _block_diag(w):
    n, c, d = w.shape
    eye = jnp.eye(n, dtype=w.dtype)
    return (eye[:, None, :, None] * w[:, :, None, :]).reshape(n * c, n * d)


def _head_norm(o, g, out_scale):
    o = o * lax.rsqrt(jnp.mean(o * o, axis=-1, keepdims=True) + LN_EPS) * g
    return o * out_scale


def _attn_prompt_kernel(scal_ref, sbt_ref, q_ref, k_ref, v_ref, g_ref, o_ref, m_sc, l_sc, acc_sc, doff_sc, ddiag_sc,
                        *, blk, hd, n_heads, out_scale):
    h = pl.program_id(1)
    i = pl.program_id(2)
    nq = pl.num_programs(2)
    slope2 = scal_ref[h]
    lam = scal_ref[n_heads]
    c1 = (hd ** -0.5) * LOG2E
    shift = int(math.log2(CHUNK))
    e = 2 * hd

    @pl.when(i == 0)
    def _():
        r = lax.broadcasted_iota(jnp.int32, (blk, blk), 0)
        c = lax.broadcasted_iota(jnp.int32, (blk, blk), 1)
        rel = r - c
        doff_sc[...] = slope2 * rel.astype(F32)
        visible = lax.shift_right_arithmetic(c, shift) <= lax.shift_right_arithmetic(r, shift)
        ddiag_sc[...] = jnp.where(visible, slope2 * jnp.abs(rel).astype(F32), -NEG_INF)

    q = q_ref[...].astype(BF16)
    m_sc[...] = jnp.full(m_sc.shape, NEG_INF, F32)
    l_sc[...] = jnp.zeros(l_sc.shape, F32)
    acc_sc[...] = jnp.zeros(acc_sc.shape, F32)

    def process(j, d_ref, sb):
        start = pl.multiple_of(j * blk, blk)
        kj = k_ref[pl.ds(start, blk), :].astype(BF16)
        vj = v_ref[pl.ds(start, blk), :].astype(BF16)
        for c in range(2):
            cols = slice(c * hd, (c + 1) * hd)
            t = _dot_nt(q[:, cols], kj[:, cols]) * c1 - d_ref[...]
            m_prev = m_sc[c]
            m_next = jnp.maximum(m_prev, jnp.max(t, axis=1, keepdims=True) - sb)
            p = jnp.exp2(t - jnp.tile(m_next + sb, (1, blk // LANES)))
            corr = jnp.exp2(m_prev - m_next)
            psum = p[:, 0:LANES]
            for w in range(1, blk // LANES):
                psum = psum + p[:, w * LANES:(w + 1) * LANES]
            l_sc[c] = corr * l_sc[c] + psum
            acc_sc[c] = acc_sc[c] * jnp.tile(corr, (1, e // LANES)) + _dot(p.astype(BF16), vj)
            m_sc[c] = m_next

    def body(j, carry):
        process(j, doff_sc, sbt_ref[h * nq + (i - j)])
        return carry

    lax.fori_loop(0, i, body, 0)
    process(i, ddiag_sc, 0.0)
    outs = [acc_sc[c] / jnp.sum(l_sc[c], axis=1, keepdims=True) for c in range(2)]
    o = outs[0] - lam * outs[1]
    o_ref[...] = _head_norm(o, g_ref[...], out_scale).astype(BF16)


def _attn_prompt(proj, slopes, lam, subln, *, n_batch, seq, n_heads, hd, q_col, k_col, v_col, out_rows, out_scale,
                 blk=512):
    assert blk % CHUNK == 0 and blk % LANES == 0 and seq % blk == 0
    nq = seq // blk
    e = 2 * hd
    scal = jnp.concatenate([slopes * LOG2E, lam.reshape(1)])
    sb_tab = (slopes[:, None] * (LOG2E * blk * jnp.arange(nq, dtype=F32))[None, :]).reshape(-1)
    smem = pl.BlockSpec(memory_space=pltpu.SMEM)
    return pl.pallas_call(
        functools.partial(_attn_prompt_kernel, blk=blk, hd=hd, n_heads=n_heads, out_scale=out_scale),
        grid=(n_batch, n_heads, nq),
        in_specs=[smem, smem,
                  pl.BlockSpec((blk, e), lambda b, h, i: (b * nq + i, q_col + h)),
                  pl.BlockSpec((seq, e), lambda b, h, i: (b, k_col + h)),
                  pl.BlockSpec((seq, e), lambda b, h, i: (b, v_col + h)),
                  pl.BlockSpec((1, e), lambda b, h, i: (0, 0))],
        out_specs=pl.BlockSpec((blk, e), lambda b, h, i: (b * nq + i, h)),
        out_shape=jax.ShapeDtypeStruct((out_rows, n_heads * e), BF16),
        scratch_shapes=[pltpu.VMEM((2, blk, LANES), F32), pltpu.VMEM((2, blk, LANES), F32),
                        pltpu.VMEM((2, blk, e), F32), pltpu.VMEM((blk, blk), F32), pltpu.VMEM((blk, blk), F32)],
        compiler_params=_params(("parallel", "parallel", "arbitrary"), 48),
    )(scal, sb_tab, proj, proj, proj, subln.reshape(1, e))


def _attn_sample_kernel(scal_ref, q_ref, kn_ref, vn_ref, kc_ref, vc_ref, g_ref, o_ref,
                        *, hd, n_heads, past, out_scale):
    h = pl.program_id(1)
    slope = scal_ref[h]
    lam = scal_ref[n_heads]
    scale = hd ** -0.5
    shift = int(math.log2(CHUNK))
    t = q_ref.shape[0]
    q = q_ref[...].astype(BF16)
    kc = kc_ref[...].astype(BF16)
    vc = vc_ref[...].astype(BF16)
    kn = kn_ref[...].astype(BF16)
    vn = vn_ref[...].astype(BF16)
    qpos = past + lax.broadcasted_iota(jnp.int32, (t, 1), 0)
    qchunk = lax.shift_right_arithmetic(qpos, shift)

    def masked_bias(kpos):
        bias = slope * jnp.abs(qpos - kpos).astype(F32)
        return bias, lax.shift_right_arithmetic(kpos, shift) <= qchunk

    bias_c, vis_c = masked_bias(lax.broadcasted_iota(jnp.int32, (1, past), 1))
    bias_n, vis_n = masked_bias(past + lax.broadcasted_iota(jnp.int32, (1, t), 1))
    outs = []
    for c in range(2):
        cols = slice(c * hd, (c + 1) * hd)
        s_c = jnp.where(vis_c, _dot_nt(q[:, cols], kc[:, cols]) * scale - bias_c, NEG_INF)
        s_n = jnp.where(vis_n, _dot_nt(q[:, cols], kn[:, cols]) * scale - bias_n, NEG_INF)
        m = jnp.maximum(jnp.max(s_c, axis=-1, keepdims=True), jnp.max(s_n, axis=-1, keepdims=True))
        p_c = jnp.exp(s_c - m)
        p_n = jnp.exp(s_n - m)
        l = jnp.sum(p_c, axis=-1, keepdims=True) + jnp.sum(p_n, axis=-1, keepdims=True)
        outs.append((_dot(p_c.astype(BF16), vc) + _dot(p_n.astype(BF16), vn)) / l)
    o = outs[0] - lam * outs[1]
    o_ref[...] = _head_norm(o, g_ref[...], out_scale).astype(BF16)


def _attn_sample(proj, cache_k, cache_v, scal, subln, *, layer, n_batch, seq, n_heads, hd, q_col, k_col, v_col,
                 row0, out_scale):
    e = 2 * hd
    past = cache_k.shape[1]
    rb0 = row0 // seq
    new = lambda col: pl.BlockSpec((seq, e), lambda b, h: (rb0 + b, col + h))
    old = pl.BlockSpec((None, past, e), lambda b, h: (layer * n_batch + b, 0, h))
    return pl.pallas_call(
        functools.partial(_attn_sample_kernel, hd=hd, n_heads=n_heads, past=past, out_scale=out_scale),
        grid=(n_batch, n_heads),
        in_specs=[pl.BlockSpec(memory_space=pltpu.SMEM), new(q_col), new(k_col), new(v_col), old, old,
                  pl.BlockSpec((1, e), lambda b, h: (0, 0))],
        out_specs=pl.BlockSpec((seq, e), lambda b, h: (b, h)),
        out_shape=jax.ShapeDtypeStruct((n_batch * seq, n_heads * e), BF16),
        compiler_params=_params(("parallel", "parallel"), 40),
    )(scal, proj, proj, proj, cache_k, cache_v, subln.reshape(1, e))


def _merge_kernel(ya_ref, yb_ref, yc_ref, wa_ref, wb_ref, wc_ref, ga_ref, gb_ref, gc_ref, bg_ref, o_ref):
    def branch(y_ref, w_ref, g_ref, r):
        return jax.nn.sigmoid(g_ref[...] + bg_ref[r:r + 1, :]) * _dot(y_ref[...], w_ref[...])

    merged = branch(ya_ref, wa_ref, ga_ref, 0) + branch(yb_ref, wb_ref, gb_ref, 1) + branch(yc_ref, wc_ref, gc_ref, 2)
    o_ref[...] = merged.astype(BF16)


def _merge(ya, yb, yc, wa, wb, wc, proj, b_gate, *, gate_col, bm=512, bn=1024):
    m = ya.shape[0]
    d = wa.shape[1]
    nj = d // bn
    y_spec = lambda y: pl.BlockSpec((bm, y.shape[1]), lambda i, j: (i, 0))
    w_spec = lambda w: pl.BlockSpec((w.shape[0], bn), lambda i, j: (0, j))
    g_spec = lambda r: pl.BlockSpec((bm, bn), lambda i, j: (i, gate_col + r * nj + j))
    return pl.pallas_call(
        _merge_kernel,
        grid=(m // bm, nj),
        in_specs=[y_spec(ya), y_spec(yb), y_spec(yc), w_spec(wa), w_spec(wb), w_spec(wc),
                  g_spec(0), g_spec(1), g_spec(2), pl.BlockSpec((3, bn), lambda i, j: (0, j))],
        out_specs=pl.BlockSpec((bm, bn), lambda i, j: (i, j)),
        out_shape=jax.ShapeDtypeStruct((m, d), BF16),
        compiler_params=_params(("parallel", "parallel"), 48),
    )(ya, yb, yc, wa, wb, wc, proj, proj, proj, b_gate)


def _ple_kernel(xb_ref, wg_ref, p_ref, wp_ref, x_ref, o_ref, ob_ref):
    gate = jax.nn.sigmoid(_dot(xb_ref[...], wg_ref[...]))
    out = x_ref[...] + gate * _dot(p_ref[...], wp_ref[...])
    o_ref[...] = out
    ob_ref[...] = out.astype(BF16)


def _ple(x, xb, pb, w_gate, w_proj, *, bm=512, bn=1024):
    m, d = x.shape
    pdim = pb.shape[1]
    tile = pl.BlockSpec((bm, bn), lambda i, j: (i, j))
    return pl.pallas_call(
        _ple_kernel,
        grid=(m // bm, d // bn),
        in_specs=[pl.BlockSpec((bm, d), lambda i, j: (i, 0)), pl.BlockSpec((d, bn), lambda i, j: (0, j)),
                  pl.BlockSpec((bm, pdim), lambda i, j: (i, 0)), pl.BlockSpec((pdim, bn), lambda i, j: (0, j)),
                  tile],
        out_specs=[tile, tile],
        out_shape=[jax.ShapeDtypeStruct((m, d), F32), jax.ShapeDtypeStruct((m, d), BF16)],
        compiler_params=_params(("parallel", "parallel"), 48),
    )(xb, w_gate, pb, w_proj, x)


def kernel(x_prompt, x_sample, cache_k, cache_v, state_s5_re, state_s5_im, state_lru, state_conv, p_prompt, p_sample, ln_g, ln_b, ffn_w_in, ffn_w_out, w_in, b_gate, s5_lam_re, s5_lam_im, s5_log_step, s5_b_re, s5_b_im, s5_c_re, s5_c_im, s5_d, s5_w_glu, s5_b_glu, diff_lambda, diff_subln, lru_conv_w, lru_conv_b, lru_w_a, lru_b_a, lru_w_x, lru_b_x, lru_lambda, w_br_a, w_br_b, w_br_c, w_o, ple_w_proj, ple_w_gate):
    bp, tp, d_model = x_prompt.shape
    bs, ts, _ = x_sample.shape
    depth = ln_g.shape[0]
    mp, ms = bp * tp, bs * ts
    m = mp + ms
    n_heads, e = cache_k.shape[3], cache_k.shape[4]
    hd = e // 2
    past = cache_k.shape[2]
    groups, p_state, grp_ch = s5_b_re.shape[1:]
    s5_w = groups * grp_ch
    att_w = n_heads * e
    lru_w = lru_lambda.shape[1]
    d_ff = ffn_w_out.shape[2]
    f_pad = -(-d_ff // 1024) * 1024
    alpha = (2 * depth) ** 0.25
    L = S5_CHUNK
    assert ts == L and tp % L == 0 and ts == LRU_ROWS and ts == CHUNK and past % CHUNK == 0
    kp = tp // L
    c_q, c_k, c_v = s5_w, s5_w + att_w, s5_w + 2 * att_w
    c_xr = s5_w + 3 * att_w
    c_gr = c_xr + lru_w
    c_gl = c_gr + lru_w

    x = jnp.concatenate([x_prompt.reshape(mp, d_model), x_sample.reshape(ms, d_model)], axis=0)
    xb = x.astype(BF16)
    cache_k2 = cache_k.reshape(depth * bs, past, att_w)
    cache_v2 = cache_v.reshape(depth * bs, past, att_w)
    slopes = jnp.exp2(-8.0 * (jnp.arange(n_heads, dtype=F32) + 1.0) / n_heads)

    w_dn_all = jnp.pad(ffn_w_out.astype(BF16), ((0, 0), (0, 0), (0, f_pad - d_ff), (0, 0)))

    def ffn(i, s, x, xb):
        hdn = _swiglu_in(xb, ffn_w_in, w_lead=(i, s), f_pad=f_pad, bm=1024, bn=256)
        f = _matmul(hdn, w_dn_all, w_lead=(i, s), bm=1024, bn=1024, bk=f_pad // 4, out_dtype=F32, vmem_mib=48)
        return _residual_ln(x, f, ln_g[i, 2 * s], ln_b[i, 2 * s], alpha=alpha, scale=0.5)

    ks, vs, s5r, s5i, lruh, convs = [], [], [], [], [], []
    for i in range(depth):
        x, xb = ffn(i, 0, x, xb)

        proj = _matmul_f32w(xb, w_in, w_lead=(i,), bm=1024, bn=512, out_dtype=F32, vmem_mib=48)

        u_p = proj[:mp, :s5_w].reshape(bp, kp, L, groups, grp_ch).transpose(3, 1, 0, 4, 2).reshape(groups, kp * bp, grp_ch * L)
        u_s = proj[mp:, :s5_w].reshape(bs, L, groups, grp_ch).transpose(2, 0, 3, 1).reshape(groups, bs, grp_ch * L)
        u_g = jnp.concatenate([u_p, u_s], axis=1)
        h0_g = jnp.concatenate([state_s5_re[i], state_s5_im[i]], axis=-1).transpose(1, 0, 2)
        mats = _s5_matrices(s5_lam_re[i], s5_lam_im[i], s5_log_step[i], s5_b_re[i], s5_b_im[i], s5_c_re[i], s5_c_im[i])
        y_g, hfin = _s5_scan(u_g, h0_g, mats, n_seq=bp, n_chunk=kp, n_single=bs)
        y_p = y_g[:, :kp * bp].reshape(groups, kp, bp, L, grp_ch).transpose(2, 1, 3, 0, 4).reshape(mp, s5_w)
        y_s = y_g[:, kp * bp:].reshape(groups, bs, L, grp_ch).transpose(1, 2, 0, 3).reshape(ms, s5_w)
        y_a = _s5_glu(jnp.concatenate([y_p, y_s], axis=0), proj, s5_d[i], s5_w_glu[i].astype(BF16), s5_b_glu[i])
        hfin = hfin.transpose(1, 0, 2)
        s5r.append((hfin[:bp, :, :p_state], hfin[bp:, :, :p_state]))
        s5i.append((hfin[:bp, :, p_state:], hfin[bp:, :, p_state:]))

        lam_init = 0.8 - 0.6 * math.exp(-0.3 * i)
        dl = diff_lambda[i].astype(F32)
        lam = jnp.exp(jnp.sum(dl[0] * dl[1])) - jnp.exp(jnp.sum(dl[2] * dl[3])) + lam_init
        scal = jnp.concatenate([slopes, lam.reshape(1)])
        acols = dict(q_col=c_q // e, k_col=c_k // e, v_col=c_v // e)
        yb_p = _attn_prompt(proj, slopes, lam, diff_subln[i], n_batch=bp, seq=tp, n_heads=n_heads, hd=hd,
                            out_rows=mp, out_scale=1.0 - lam_init, **acols)
        yb_s = _attn_sample(proj, cache_k2, cache_v2, scal, diff_subln[i], layer=i, n_batch=bs, seq=ts,
                            n_heads=n_heads, hd=hd, row0=mp, out_scale=1.0 - lam_init, **acols)
        y_b = jnp.concatenate([yb_p, yb_s], axis=0)

        h0_all = jnp.concatenate([jnp.zeros((bp, lru_w), F32), state_lru[i]], axis=0).reshape(bp + bs, 1, lru_w)
        c0_all = jnp.concatenate([jnp.zeros((bp,) + state_conv.shape[2:], F32), state_conv[i]], axis=0)
        sp = jax.nn.softplus(-lru_lambda[i].astype(F32))
        y_c, h_all, c_all = _rglru(proj, h0_all, c0_all, lru_conv_w[i], lru_conv_b[i],
                                   _block_diag(lru_w_a[i]).astype(BF16), lru_b_a[i].reshape(-1),
                                   _block_diag(lru_w_x[i]).astype(BF16), lru_b_x[i].reshape(-1), sp,
                                   m=m, width=lru_w, x_col=c_xr // lru_w, gate_col=c_gr // lru_w,
                                   n_chain_seq=bp, blocks_per_seq=tp // LRU_ROWS)
        lruh.append((h_all[:bp, 0], h_all[bp:, 0]))
        convs.append((c_all[:bp], c_all[bp:]))

        merged = _merge(y_a, y_b, y_c, w_br_a[i].astype(BF16), w_br_b[i].astype(BF16), w_br_c[i].astype(BF16),
                        proj, b_gate[i], gate_col=c_gl // 1024)
        mix = _matmul_f32w(merged, w_o, w_lead=(i,), bm=1024, bn=512, out_dtype=F32, vmem_mib=48)
        x, xb = _residual_ln(x, mix, ln_g[i, 1], ln_b[i, 1], alpha=alpha, scale=1.0)

        x, xb = ffn(i, 1, x, xb)

        pb = jnp.concatenate([p_prompt[i].reshape(mp, -1), p_sample[i].reshape(ms, -1)], axis=0).astype(BF16)
        x, xb = _ple(x, xb, pb, ple_w_gate[i].astype(BF16), ple_w_proj[i].astype(BF16))

        ks.append((proj[:mp, c_k:c_v].reshape(bp, tp, n_heads, e), proj[mp:, c_k:c_v].reshape(bs, ts, n_heads, e)))
        vs.append((proj[:mp, c_v:c_xr].reshape(bp, tp, n_heads, e), proj[mp:, c_v:c_xr].reshape(bs, ts, n_heads, e)))

    stack = lambda pairs, which: jnp.stack([pr[which] for pr in pairs])
    return (x[:mp].reshape(bp, tp, d_model), x[mp:].reshape(bs, ts, d_model),
            stack(ks, 0), stack(vs, 0), stack(s5r, 0), stack(s5i, 0), stack(lruh, 0), stack(convs, 0),
            stack(ks, 1), stack(vs, 1), stack(s5r, 1), stack(s5i, 1), stack(lruh, 1), stack(convs, 1))
```

```python
import functools
import math

import jax
import jax.numpy as jnp
from jax import lax
from jax.experimental import pallas as pl
from jax.experimental.pallas import tpu as pltpu

F32 = jnp.float32
BF16 = jnp.bfloat16

CHUNK = 64
LRU_C = 8.0
LN_EPS = 1e-5
NEG_INF = -1e30
S5_CHUNK = 64
LRU_ROWS = 64

LANES = 128
LOG2E = math.log2(math.e)

MIB = 1024 * 1024


def _params(semantics, vmem_mib):
    return pltpu.CompilerParams(dimension_semantics=semantics, vmem_limit_bytes=vmem_mib * MIB)


def _dot(a, b):
    return jnp.dot(a, b, preferred_element_type=F32)


def _dot_nt(a, b):
    return lax.dot_general(a, b, (((1,), (1,)), ((), ())), preferred_element_type=F32)


def _split_bf16(a):
    hi = a.astype(BF16)
    lo = (a - hi.astype(F32)).astype(BF16)
    return hi, lo


def _dot3(a, b):
    a_hi, a_lo = _split_bf16(a)
    b_hi, b_lo = _split_bf16(b)
    return _dot(a_hi, b_hi) + (_dot(a_lo, b_hi) + _dot(a_hi, b_lo))


def _mm_res_kernel(x_ref, w_ref, r_ref, o_ref, *, nk, alpha, scale):
    part = _dot(x_ref[...], w_ref[...])
    k = pl.program_id(2)

    @pl.when(k == 0)
    def _():
        o_ref[...] = part

    @pl.when(k > 0)
    def _():
        o_ref[...] += part

    @pl.when(k == nk - 1)
    def _():
        o_ref[...] = alpha * r_ref[...] + scale * o_ref[...]


def _matmul_res(x, w, res, *, w_lead=(), bm, bn, bk, alpha, scale, vmem_mib):
    m, kdim = x.shape
    n = w.shape[-1]
    nk = kdim // bk
    assert w.shape[-2] == kdim and m % bm == 0 and n % bn == 0 and kdim % bk == 0 and nk > 1
    tile = pl.BlockSpec((bm, bn), lambda i, j, k: (i, j))
    return pl.pallas_call(
        functools.partial(_mm_res_kernel, nk=nk, alpha=alpha, scale=scale),
        grid=(m // bm, n // bn, nk),
        in_specs=[pl.BlockSpec((bm, bk), lambda i, j, k: (i, k)),
                  pl.BlockSpec((None,) * len(w_lead) + (bk, bn), lambda i, j, k: w_lead + (k, j)), tile],
        out_specs=tile,
        out_shape=jax.ShapeDtypeStruct((m, n), F32),
        compiler_params=_params(("parallel", "parallel", "arbitrary"), vmem_mib),
    )(x, w, res)


def _mm_f32w_kernel(x_ref, w_ref, *rest, alpha, scale):
    o_ref, w_sc = rest[-2:]

    @pl.when(pl.program_id(1) == 0)
    def _():
        w_sc[...] = w_ref[...].astype(BF16)

    acc = _dot(x_ref[...], w_sc[...])
    if len(rest) == 3:
        acc = alpha * rest[0][...] + scale * acc
    o_ref[...] = acc.astype(o_ref.dtype)


def _matmul_f32w(x, w, res=None, *, w_lead=(), bm, bn, out_dtype, vmem_mib, alpha=1.0, scale=1.0):
    m, kdim = x.shape
    n = w.shape[-1]
    assert w.shape[-2] == kdim and m % bm == 0 and n % bn == 0
    tile = pl.BlockSpec((bm, bn), lambda j, i: (i, j))
    extra = [] if res is None else [res]
    return pl.pallas_call(
        functools.partial(_mm_f32w_kernel, alpha=alpha, scale=scale),
        grid=(n // bn, m // bm),
        in_specs=[pl.BlockSpec((bm, kdim), lambda j, i: (i, 0)),
                  pl.BlockSpec((None,) * len(w_lead) + (kdim, bn), lambda j, i: w_lead + (0, j))]
                 + [tile] * len(extra),
        out_specs=tile,
        out_shape=jax.ShapeDtypeStruct((m, n), out_dtype),
        scratch_shapes=[pltpu.VMEM((kdim, bn), BF16)],
        compiler_params=_params(("parallel", "arbitrary"), vmem_mib),
    )(x, w, *extra)


def _swiglu_kernel(x_ref, wg_ref, wu_ref, o_ref, wg_sc, wu_sc, *, n_real):
    j = pl.program_id(0)

    @pl.when(jnp.logical_and(pl.program_id(1) == 0, j < n_real))
    def _():
        wg_sc[...] = wg_ref[...].astype(BF16)
        wu_sc[...] = wu_ref[...].astype(BF16)

    @pl.when(j < n_real)
    def _():
        x = x_ref[...]
        g = _dot(x, wg_sc[...])
        u = _dot(x, wu_sc[...])
        o_ref[...] = (g * jax.nn.sigmoid(g) * u).astype(o_ref.dtype)

    @pl.when(j >= n_real)
    def _():
        o_ref[...] = jnp.zeros(o_ref.shape, o_ref.dtype)


def _swiglu_in(xb, w_gu, *, w_lead, f_pad, bm, bn):
    m, kdim = xb.shape
    f = w_gu.shape[-1] // 2
    n_real = f // bn
    nj = f_pad // bn
    assert f % bn == 0 and f_pad % bn == 0 and m % bm == 0
    lead = (None,) * len(w_lead)
    col = lambda j: jnp.minimum(j, n_real - 1)
    return pl.pallas_call(
        functools.partial(_swiglu_kernel, n_real=n_real),
        grid=(nj, m // bm),
        in_specs=[pl.BlockSpec((bm, kdim), lambda j, i: (i, 0)),
                  pl.BlockSpec(lead + (kdim, bn), lambda j, i: w_lead + (0, col(j))),
                  pl.BlockSpec(lead + (kdim, bn), lambda j, i: w_lead + (0, col(j) + n_real))],
        out_specs=pl.BlockSpec((bm, bn), lambda j, i: (i, j)),
        out_shape=jax.ShapeDtypeStruct((m, f_pad), BF16),
        scratch_shapes=[pltpu.VMEM((kdim, bn), BF16), pltpu.VMEM((kdim, bn), BF16)],
        compiler_params=_params(("parallel", "arbitrary"), 48),
    )(xb, w_gu, w_gu)


def _ln_kernel(y_ref, g_ref, b_ref, o_ref, ob_ref):
    y = y_ref[...]
    mu = jnp.mean(y, axis=-1, keepdims=True)
    d = y - mu
    var = jnp.mean(d * d, axis=-1, keepdims=True)
    out = d * lax.rsqrt(var + LN_EPS) * g_ref[...] + b_ref[...]
    o_ref[...] = out
    ob_ref[...] = out.astype(BF16)


def _layer_norm(y, g, b, *, bm=256):
    m, d = y.shape
    row = pl.BlockSpec((bm, d), lambda i: (i, 0))
    vec = pl.BlockSpec((1, d), lambda i: (0, 0))
    return pl.pallas_call(
        _ln_kernel,
        grid=(m // bm,),
        in_specs=[row, vec, vec],
        out_specs=[row, row],
        out_shape=[jax.ShapeDtypeStruct((m, d), F32), jax.ShapeDtypeStruct((m, d), BF16)],
        compiler_params=_params(("parallel",), 40),
    )(y, g.reshape(1, d), b.reshape(1, d))


def _s5_kernel(u_ref, h0_ref, min_ref, minsw_ref, kseq_ref, mout_ref, dec_ref, y_ref, hfin_ref,
               s_sc, ssw_sc, hprev_sc, toep_sc, *, n_seq, n_chunk, n_single):
    ch, seq_w = kseq_ref.shape
    width = toep_sc.shape[1]
    per_tile = LANES // ch
    kseq = kseq_ref[...]
    for b in range(per_tile):
        rot = kseq if b == 0 else pltpu.roll(kseq, seq_w - ch * b, axis=1)
        for a in range(width // LANES):
            e = a * per_tile + b
            toep_sc[e * ch:(e + 1) * ch, :] = rot[:, a * LANES:a * LANES + width]

    u = u_ref[...]
    u_hi, u_lo = _split_bf16(u)

    def dot3_u(w):
        w_hi, w_lo = _split_bf16(w)
        return _dot(u_hi, w_hi) + (_dot(u_lo, w_hi) + _dot(u_hi, w_lo))

    s_sc[...] = dot3_u(min_ref[...])
    ssw_sc[...] = dot3_u(minsw_ref[...])
    a1 = dec_ref[0:1, :]
    a2 = dec_ref[1:2, :]
    a2sw = dec_ref[2:3, :]

    h = jnp.zeros((n_seq, s_sc.shape[1]), F32)
    hsw = h
    for k in range(n_chunk):
        rows = slice(k * n_seq, (k + 1) * n_seq)
        hprev_sc[rows, :] = h
        h, hsw = (a1 * h + a2 * hsw + s_sc[rows, :], a1 * hsw + a2sw * h + ssw_sc[rows, :])
    n_chain = n_chunk * n_seq
    hfin_ref[0:n_seq, :] = h
    h0 = h0_ref[...]
    h0sw = pltpu.roll(h0, h0.shape[1] // 2, axis=1)
    hprev_sc[n_chain:n_chain + n_single, :] = h0
    hfin_ref[n_seq:n_seq + n_single, :] = a1 * h0 + a2 * h0sw + s_sc[n_chain:n_chain + n_single, :]

    y_ref[...] = dot3_u(toep_sc[...]) + _dot3(hprev_sc[...], mout_ref[...])


def _s5_scan(u_g, h0_g, mats, *, n_seq, n_chunk, n_single):
    m_in, m_in_sw, kseq, m_out, dec = mats
    g, rows, width = u_g.shape
    p2 = m_in.shape[2]
    n_out = n_seq + n_single
    grp = lambda *shape: pl.BlockSpec((None,) + shape, lambda i: (i,) + (0,) * len(shape))
    return pl.pallas_call(
        functools.partial(_s5_kernel, n_seq=n_seq, n_chunk=n_chunk, n_single=n_single),
        grid=(g,),
        in_specs=[grp(rows, width), grp(n_single, p2), grp(width, p2), grp(width, p2),
                  grp(*kseq.shape[1:]), grp(p2, width), grp(3, p2)],
        out_specs=[grp(rows, width), grp(n_out, p2)],
        out_shape=[jax.ShapeDtypeStruct((g, rows, width), F32),
                   jax.ShapeDtypeStruct((g, n_out, p2), F32)],
        scratch_shapes=[pltpu.VMEM((rows, p2), F32), pltpu.VMEM((rows, p2), F32),
                        pltpu.VMEM((rows, p2), F32), pltpu.VMEM((width, width), F32)],
        compiler_params=_params(("parallel",), 40),
    )(u_g, h0_g, m_in, m_in_sw, kseq, m_out, dec)


def _s5_matrices(lam_re, lam_im, log_step, b_re, b_im, c_re, c_im):
    hp = lax.Precision.HIGHEST
    L = S5_CHUNK
    g, p = lam_re.shape
    ch = b_re.shape[2]
    lam = lax.complex(lam_re.astype(F32), lam_im.astype(F32))
    step = jnp.exp(log_step.astype(F32))[:, None]
    lam_step = lam * step
    lam_bar = jnp.exp(lam_step)
    b_bar = ((lam_bar - 1.0) / lam)[..., None] * lax.complex(b_re.astype(F32), b_im.astype(F32))
    c_mat = lax.complex(c_re.astype(F32), c_im.astype(F32))
    d = jnp.arange(L + 1, dtype=F32)
    pw = jnp.exp(lam_step[None] * d[:, None, None])
    w_in = pw[:L].transpose(1, 0, 2)[:, :, None, :] * b_bar.transpose(0, 2, 1)[:, None, :, :]
    w_in = w_in.reshape(g, L * ch, p)
    m_in = jnp.concatenate([jnp.real(w_in), jnp.imag(w_in)], axis=-1)
    m_in_sw = jnp.concatenate([jnp.imag(w_in), jnp.real(w_in)], axis=-1)
    w_out = pw[1:].transpose(1, 2, 0)[:, :, :, None] * c_mat.transpose(0, 2, 1)[:, :, None, :]
    w_out = w_out.reshape(g, p, L * ch)
    m_out = jnp.concatenate([jnp.real(w_out), -jnp.imag(w_out)], axis=1)
    cp = c_mat[None] * pw[:L][:, :, None, :]
    kd = (jnp.einsum('dgcp,gpe->dgce', jnp.real(cp), jnp.real(b_bar), precision=hp)
          - jnp.einsum('dgcp,gpe->dgce', jnp.imag(cp), jnp.imag(b_bar), precision=hp))
    kseq = jnp.pad(kd.transpose(1, 3, 0, 2), ((0, 0), (0, 0), (L - 1, 1), (0, 0))).reshape(g, ch, 2 * L * ch)
    pl_ = pw[L]
    dec = jnp.stack([jnp.concatenate([jnp.real(pl_), jnp.real(pl_)], -1),
                     jnp.concatenate([-jnp.imag(pl_), jnp.imag(pl_)], -1),
                     jnp.concatenate([jnp.imag(pl_), -jnp.imag(pl_)], -1)], axis=1)
    return m_in, m_in_sw, kseq, m_out, dec


def _glu_kernel(y1_ref, y2_ref, u_ref, d_ref, w_ref, b_ref, o_ref, *, n_first):
    def run(y_ref):
        y = jax.nn.gelu(y_ref[...] + d_ref[...] * u_ref[...])
        z = _dot(y.astype(BF16), w_ref[...]) + b_ref[...]
        o_ref[...] = (y * jax.nn.sigmoid(z)).astype(BF16)

    pl.when(pl.program_id(0) < n_first)(lambda: run(y1_ref))
    pl.when(pl.program_id(0) >= n_first)(lambda: run(y2_ref))


def _s5_glu(y1, y2, proj, d_skip, w_glu, b_glu, *, bm=512):
    (m1, w), m2 = y1.shape, y2.shape[0]
    assert m1 % bm == 0 and m2 % bm == 0
    n_first = m1 // bm
    row = pl.BlockSpec((bm, w), lambda i: (i, 0))
    vec = pl.BlockSpec((1, w), lambda i: (0, 0))
    return pl.pallas_call(
        functools.partial(_glu_kernel, n_first=n_first),
        grid=((m1 + m2) // bm,),
        in_specs=[pl.BlockSpec((bm, w), lambda i: (jnp.minimum(i, n_first - 1), 0)),
                  pl.BlockSpec((bm, w), lambda i: (jnp.maximum(i - n_first, 0), 0)),
                  row, vec, pl.BlockSpec((w, w), lambda i: (0, 0)), vec],
        out_specs=row,
        out_shape=jax.ShapeDtypeStruct((m1 + m2, w), BF16),
        compiler_params=_params(("parallel",), 32),
    )(y1, y2, proj, d_skip.reshape(1, w), w_glu, b_glu.reshape(1, w))


def _lru_kernel(x_ref, gate_ref, h0_ref, c0_ref, cw_ref, cb_ref, wa_ref, ba_ref, wx_ref, bx_ref, sp_ref,
                y_ref, hout_ref, cout_ref, xpad_sc, a_sc, b_sc, hs_sc, h_sc,
                *, rows, n_chain_blocks, blocks_per_seq, conv_width):
    blk = pl.program_id(0)
    tail = conv_width - 1
    base = 8
    is_start = jnp.logical_or(blk >= n_chain_blocks, blk % blocks_per_seq == 0)

    @pl.when(is_start)
    def _():
        h_sc[...] = h0_ref[...]
        xpad_sc[base - tail:base, :] = c0_ref[...]

    x = x_ref[...]
    xpad_sc[base:base + rows, :] = x
    xc = cb_ref[...]
    for j in range(conv_width):
        off = base - tail + j
        xc = xc + cw_ref[j:j + 1, :] * xpad_sc[off:off + rows, :]
    new_tail = x_ref[rows - tail:rows, :]
    xpad_sc[base - tail:base, :] = new_tail
    cout_ref[...] = new_tail

    xcb = xc.astype(BF16)
    r = jax.nn.sigmoid(_dot(xcb, wa_ref[...]) + ba_ref[...])
    ig = jax.nn.sigmoid(_dot(xcb, wx_ref[...]) + bx_ref[...])
    log_a = -LRU_C * r * sp_ref[...]
    a = jnp.exp(log_a)
    mult = jnp.sqrt(-jnp.tanh(log_a) * (a * a + 1.0))
    a_sc[...] = a
    b_sc[...] = mult * (ig * xc)

    def step(t, h):
        h = a_sc[pl.ds(t, 1), :] * h + b_sc[pl.ds(t, 1), :]
        hs_sc[pl.ds(t, 1), :] = h
        return h

    h = lax.fori_loop(0, rows, step, h_sc[...], unroll=8)
    h_sc[...] = h
    hout_ref[...] = h
    y_ref[...] = (hs_sc[...] * jax.nn.gelu(gate_ref[...])).astype(BF16)


def _rglru(proj, h0_all, c0_all, conv_w, conv_b, wa_bd, b_a, wx_bd, b_x, sp, *, m, width, x_col, gate_col,
           n_chain_seq, blocks_per_seq):
    rows = LRU_ROWS
    nblk = m // rows
    n_chain_blocks = n_chain_seq * blocks_per_seq
    n_seq = h0_all.shape[0]
    cw = conv_w.shape[0]

    def seq_of(i):
        return jnp.where(i < n_chain_blocks, i // blocks_per_seq, n_chain_seq + i - n_chain_blocks)

    vec = pl.BlockSpec((1, width), lambda i: (0, 0))
    mat = pl.BlockSpec((width, width), lambda i: (0, 0))
    hspec = pl.BlockSpec((None, 1, width), lambda i: (seq_of(i), 0, 0))
    cspec = pl.BlockSpec((None, cw - 1, width), lambda i: (seq_of(i), 0, 0))
    return pl.pallas_call(
        functools.partial(_lru_kernel, rows=rows, n_chain_blocks=n_chain_blocks,
                          blocks_per_seq=blocks_per_seq, conv_width=cw),
        grid=(nblk,),
        in_specs=[pl.BlockSpec((rows, width), lambda i: (i, x_col)),
                  pl.BlockSpec((rows, width), lambda i: (i, gate_col)),
                  hspec, cspec, pl.BlockSpec((cw, width), lambda i: (0, 0)), vec, mat, vec, mat, vec, vec],
        out_specs=[pl.BlockSpec((rows, width), lambda i: (i, 0)), hspec, cspec],
        out_shape=[jax.ShapeDtypeStruct((m, width), BF16),
                   jax.ShapeDtypeStruct((n_seq, 1, width), F32),
                   jax.ShapeDtypeStruct((n_seq, cw - 1, width), F32)],
        scratch_shapes=[pltpu.VMEM((rows + 8, width), F32), pltpu.VMEM((rows, width), F32),
                        pltpu.VMEM((rows, width), F32), pltpu.VMEM((rows, width), F32),
                        pltpu.VMEM((1, width), F32)],
        compiler_params=_params(("arbitrary",), 32),
    )(proj, proj, h0_all, c0_all, conv_w, conv_b.reshape(1, width), wa_bd, b_a.reshape(1, width),
      wx_bd, b_x.reshape(1, width), sp.reshape(1, width))


def _block_diag(w):
    n, c, d = w.shape
    eye = jnp.eye(n, dtype=w.dtype)
    return (eye[:, None, :, None] * w[:, :, None, :]).reshape(n * c, n * d)


def _head_norm(o, g, out_scale):
    o = o * lax.rsqrt(jnp.mean(o * o, axis=-1, keepdims=True) + LN_EPS) * g
    return o * out_scale


def _attn_prompt_kernel(scal_ref, sbt_ref, q_ref, k_ref, v_ref, g_ref, o_ref, m_sc, l_sc, acc_sc, doff_sc, ddiag_sc,
                        *, blk, hd, n_heads, out_scale):
    h = pl.program_id(1)
    i = pl.program_id(2)
    nq = pl.num_programs(2)
    slope2 = scal_ref[h]
    lam = scal_ref[n_heads]
    c1 = (hd ** -0.5) * LOG2E
    shift = int(math.log2(CHUNK))
    e = 2 * hd

    @pl.when(i == 0)
    def _():
        r = lax.broadcasted_iota(jnp.int32, (blk, blk), 0)
        c = lax.broadcasted_iota(jnp.int32, (blk, blk), 1)
        rel = r - c
        doff_sc[...] = slope2 * rel.astype(F32)
        visible = lax.shift_right_arithmetic(c, shift) <= lax.shift_right_arithmetic(r, shift)
        ddiag_sc[...] = jnp.where(visible, slope2 * jnp.abs(rel).astype(F32), -NEG_INF)

    q = q_ref[...].astype(BF16)
    m_sc[...] = jnp.full(m_sc.shape, NEG_INF, F32)
    l_sc[...] = jnp.zeros(l_sc.shape, F32)
    acc_sc[...] = jnp.zeros(acc_sc.shape, F32)

    def process(j, d_ref, sb):
        start = pl.multiple_of(j * blk, blk)
        kj = k_ref[pl.ds(start, blk), :].astype(BF16)
        vj = v_ref[pl.ds(start, blk), :].astype(BF16)
        for c in range(2):
            cols = slice(c * hd, (c + 1) * hd)
            t = _dot_nt(q[:, cols], kj[:, cols]) * c1 - d_ref[...]
            m_prev = m_sc[c]
            m_next = jnp.maximum(m_prev, jnp.max(t, axis=1, keepdims=True) - sb)
            p = jnp.exp2(t - jnp.tile(m_next + sb, (1, blk // LANES)))
            corr = jnp.exp2(m_prev - m_next)
            psum = p[:, 0:LANES]
            for w in range(1, blk // LANES):
                psum = psum + p[:, w * LANES:(w + 1) * LANES]
            l_sc[c] = corr * l_sc[c] + psum
            acc_sc[c] = acc_sc[c] * jnp.tile(corr, (1, e // LANES)) + _dot(p.astype(BF16), vj)
            m_sc[c] = m_next

    def body(j, carry):
        process(j, doff_sc, sbt_ref[h * nq + (i - j)])
        return carry

    lax.fori_loop(0, i, body, 0)
    process(i, ddiag_sc, 0.0)
    outs = [acc_sc[c] / jnp.sum(l_sc[c], axis=1, keepdims=True) for c in range(2)]
    o = outs[0] - lam * outs[1]
    o_ref[...] = _head_norm(o, g_ref[...], out_scale).astype(BF16)


def _attn_prompt(proj, slopes, lam, subln, *, n_batch, seq, n_heads, hd, q_col, k_col, v_col, out_rows, out_scale,
                 blk=512):
    assert blk % CHUNK == 0 and blk % LANES == 0 and seq % blk == 0
    nq = seq // blk
    e = 2 * hd
    scal = jnp.concatenate([slopes * LOG2E, lam.reshape(1)])
    sb_tab = (slopes[:, None] * (LOG2E * blk * jnp.arange(nq, dtype=F32))[None, :]).reshape(-1)
    smem = pl.BlockSpec(memory_space=pltpu.SMEM)
    return pl.pallas_call(
        functools.partial(_attn_prompt_kernel, blk=blk, hd=hd, n_heads=n_heads, out_scale=out_scale),
        grid=(n_batch, n_heads, nq),
        in_specs=[smem, smem,
                  pl.BlockSpec((blk, e), lambda b, h, i: (b * nq + i, q_col + h)),
                  pl.BlockSpec((seq, e), lambda b, h, i: (b, k_col + h)),
                  pl.BlockSpec((seq, e), lambda b, h, i: (b, v_col + h)),
                  pl.BlockSpec((1, e), lambda b, h, i: (0, 0))],
        out_specs=pl.BlockSpec((blk, e), lambda b, h, i: (b * nq + i, h)),
        out_shape=jax.ShapeDtypeStruct((out_rows, n_heads * e), BF16),
        scratch_shapes=[pltpu.VMEM((2, blk, LANES), F32), pltpu.VMEM((2, blk, LANES), F32),
                        pltpu.VMEM((2, blk, e), F32), pltpu.VMEM((blk, blk), F32), pltpu.VMEM((blk, blk), F32)],
        compiler_params=_params(("parallel", "parallel", "arbitrary"), 48),
    )(scal, sb_tab, proj, proj, proj, subln.reshape(1, e))


def _attn_sample_kernel(scal_ref, *refs, hd, n_heads, n_parts, past, kb, out_scale):
    q_refs, kn_refs, vn_refs = refs[:n_parts], refs[n_parts:2 * n_parts], refs[2 * n_parts:3 * n_parts]
    kc_ref, vc_ref, g_ref, _, o_ref, m_sc, l_sc, acc_sc = refs[3 * n_parts:]
    j = pl.program_id(1)
    lam = scal_ref[n_heads]
    scale = hd ** -0.5
    shift = int(math.log2(CHUNK))
    e = 2 * hd
    t = o_ref.shape[0]
    per_part = n_heads // n_parts
    qpos = past + lax.broadcasted_iota(jnp.int32, (t, 1), 0)

    @pl.when(j == 0)
    def _():
        m_sc[...] = jnp.full(m_sc.shape, NEG_INF, F32)
        l_sc[...] = jnp.zeros(l_sc.shape, F32)
        acc_sc[...] = jnp.zeros(acc_sc.shape, F32)

    def head_cols(part_refs, h):
        off = (h % per_part) * e
        return part_refs[h // per_part][:, off:off + e]

    def update(idx, s, v):
        m_prev = m_sc[idx]
        m_next = jnp.maximum(m_prev, jnp.max(s, axis=1, keepdims=True))
        p = jnp.exp(s - m_next[:, :1])
        corr = jnp.exp(m_prev - m_next)
        l_sc[idx] = corr * l_sc[idx] + jnp.sum(p, axis=1, keepdims=True)
        acc_sc[idx] = acc_sc[idx] * corr[:, :1] + _dot(p.astype(BF16), v)
        m_sc[idx] = m_next

    kpos = j * kb + lax.broadcasted_iota(jnp.int32, (1, kb), 1)
    dist = (qpos - kpos).astype(F32)
    for h in range(n_heads):
        bias = scal_ref[h] * dist
        q = head_cols(q_refs, h).astype(BF16)
        kh = kc_ref[:, h, :].astype(BF16)
        vh = vc_ref[:, h, :].astype(BF16)
        for c in range(2):
            cols = slice(c * hd, (c + 1) * hd)
            update(2 * h + c, _dot_nt(q[:, cols], kh[:, cols]) * scale - bias, vh)

    @pl.when(j == pl.num_programs(1) - 1)
    def _():
        kposn = past + lax.broadcasted_iota(jnp.int32, (1, t), 1)
        distn = jnp.abs(qpos - kposn).astype(F32)
        visible = lax.shift_right_arithmetic(kposn, shift) <= lax.shift_right_arithmetic(qpos, shift)
        for h in range(n_heads):
            bias = scal_ref[h] * distn
            q = head_cols(q_refs, h).astype(BF16)
            kh = head_cols(kn_refs, h).astype(BF16)
            vh = head_cols(vn_refs, h).astype(BF16)
            outs = []
            for c in range(2):
                cols = slice(c * hd, (c + 1) * hd)
                s = jnp.where(visible, _dot_nt(q[:, cols], kh[:, cols]) * scale - bias, NEG_INF)
                update(2 * h + c, s, vh)
                outs.append(acc_sc[2 * h + c] / l_sc[2 * h + c][:, :1])
            o = outs[0] - lam * outs[1]
            o_ref[:, h * e:(h + 1) * e] = _head_norm(o, g_ref[...], out_scale).astype(BF16)


def _attn_sample(proj, cache_k, cache_v, scal, subln, y_buf, *, layer, n_batch, seq, n_heads, hd, q_off, k_off,
                 v_off, row0, out_scale, kb=512):
    assert y_buf.shape[1] == n_heads * 2 * hd and y_buf.dtype == BF16
    e = 2 * hd
    att_w = n_heads * e
    past = cache_k.shape[2]
    wide = math.gcd(math.gcd(q_off, k_off), math.gcd(v_off, att_w))
    n_parts = att_w // wide
    assert past % kb == 0 and wide % e == 0 and row0 % seq == 0
    rb0 = row0 // seq
    new = lambda off: [pl.BlockSpec((seq, wide), lambda b, j, cb=off // wide + part: (rb0 + b, cb))
                       for part in range(n_parts)]
    old = pl.BlockSpec((None, None, kb, n_heads, e), lambda b, j: (layer, b, j, 0, 0))
    stat = pltpu.VMEM((2 * n_heads, seq, LANES), F32)
    return pl.pallas_call(
        functools.partial(_attn_sample_kernel, hd=hd, n_heads=n_heads, n_parts=n_parts, past=past, kb=kb,
                          out_scale=out_scale),
        grid=(n_batch, past // kb),
        in_specs=[pl.BlockSpec(memory_space=pltpu.SMEM)] + new(q_off) + new(k_off) + new(v_off)
                 + [old, old, pl.BlockSpec((1, e), lambda b, j: (0, 0)), pl.BlockSpec(memory_space=pl.ANY)],
        out_specs=pl.BlockSpec((seq, att_w), lambda b, j: (rb0 + b, 0)),
        out_shape=jax.ShapeDtypeStruct(y_buf.shape, BF16),
        input_output_aliases={3 * n_parts + 4: 0},
        scratch_shapes=[stat, stat, pltpu.VMEM((2 * n_heads, seq, e), F32)],
        compiler_params=_params(("parallel", "arbitrary"), 40),
    )(scal, *([proj] * (3 * n_parts)), cache_k, cache_v, subln.reshape(1, e), y_buf)


def _merge_kernel(ya_ref, yb_ref, yc_ref, wa_ref, wb_ref, wc_ref, ga_ref, gb_ref, gc_ref, bg_ref, o_ref):
    def branch(y_ref, w_ref, g_ref, r):
        return jax.nn.sigmoid(g_ref[...] + bg_ref[r:r + 1, :]) * _dot(y_ref[...], w_ref[...])

    merged = branch(ya_ref, wa_ref, ga_ref, 0) + branch(yb_ref, wb_ref, gb_ref, 1) + branch(yc_ref, wc_ref, gc_ref, 2)
    o_ref[...] = merged.astype(BF16)


def _merge(ya, yb, yc, wa, wb, wc, proj, b_gate, *, gate_col, bm=512, bn=1024):
    m = ya.shape[0]
    d = wa.shape[1]
    nj = d // bn
    y_spec = lambda y: pl.BlockSpec((bm, y.shape[1]), lambda i, j: (i, 0))
    w_spec = lambda w: pl.BlockSpec((w.shape[0], bn), lambda i, j: (0, j))
    g_spec = lambda r: pl.BlockSpec((bm, bn), lambda i, j: (i, gate_col + r * nj + j))
    return pl.pallas_call(
        _merge_kernel,
        grid=(m // bm, nj),
        in_specs=[y_spec(ya), y_spec(yb), y_spec(yc), w_spec(wa), w_spec(wb), w_spec(wc),
                  g_spec(0), g_spec(1), g_spec(2), pl.BlockSpec((3, bn), lambda i, j: (0, j))],
        out_specs=pl.BlockSpec((bm, bn), lambda i, j: (i, j)),
        out_shape=jax.ShapeDtypeStruct((m, d), BF16),
        compiler_params=_params(("parallel", "parallel"), 48),
    )(ya, yb, yc, wa, wb, wc, proj, proj, proj, b_gate)


def _ple_kernel(xb_ref, wg_ref, p_ref, wp_ref, x_ref, o_ref, ob_ref):
    gate = jax.nn.sigmoid(_dot(xb_ref[...], wg_ref[...]))
    out = x_ref[...] + gate * _dot(p_ref[...], wp_ref[...])
    o_ref[...] = out
    ob_ref[...] = out.astype(BF16)


def _ple(x, xb, pb, w_gate, w_proj, *, bm=512, bn=1024):
    m, d = x.shape
    pdim = pb.shape[1]
    tile = pl.BlockSpec((bm, bn), lambda i, j: (i, j))
    return pl.pallas_call(
        _ple_kernel,
        grid=(m // bm, d // bn),
        in_specs=[pl.BlockSpec((bm, d), lambda i, j: (i, 0)), pl.BlockSpec((d, bn), lambda i, j: (0, j)),
                  pl.BlockSpec((bm, pdim), lambda i, j: (i, 0)), pl.BlockSpec((pdim, bn), lambda i, j: (0, j)),
                  tile],
        out_specs=[tile, tile],
        out_shape=[jax.ShapeDtypeStruct((m, d), F32), jax.ShapeDtypeStruct((m, d), BF16)],
        compiler_params=_params(("parallel", "parallel"), 48),
    )(xb, w_gate, pb, w_proj, x)


def kernel(x_prompt, x_sample, cache_k, cache_v, state_s5_re, state_s5_im, state_lru, state_conv, p_prompt, p_sample, ln_g, ln_b, ffn_w_in, ffn_w_out, w_in, b_gate, s5_lam_re, s5_lam_im, s5_log_step, s5_b_re, s5_b_im, s5_c_re, s5_c_im, s5_d, s5_w_glu, s5_b_glu, diff_lambda, diff_subln, lru_conv_w, lru_conv_b, lru_w_a, lru_b_a, lru_w_x, lru_b_x, lru_lambda, w_br_a, w_br_b, w_br_c, w_o, ple_w_proj, ple_w_gate):
    bp, tp, d_model = x_prompt.shape
    bs, ts, _ = x_sample.shape
    depth = ln_g.shape[0]
    mp, ms = bp * tp, bs * ts
    m = mp + ms
    n_heads, e = cache_k.shape[3], cache_k.shape[4]
    hd = e // 2
    past = cache_k.shape[2]
    groups, p_state, grp_ch = s5_b_re.shape[1:]
    s5_w = groups * grp_ch
    att_w = n_heads * e
    lru_w = lru_lambda.shape[1]
    d_ff = ffn_w_out.shape[2]
    f_pad = -(-d_ff // 1024) * 1024
    alpha = (2 * depth) ** 0.25
    L = S5_CHUNK
    assert ts == L and tp % L == 0 and ts == LRU_ROWS and ts == CHUNK and past % CHUNK == 0
    kp = tp // L
    c_q, c_k, c_v = s5_w, s5_w + att_w, s5_w + 2 * att_w
    c_xr = s5_w + 3 * att_w
    c_gr = c_xr + lru_w
    c_gl = c_gr + lru_w

    x = jnp.concatenate([x_prompt.reshape(mp, d_model), x_sample.reshape(ms, d_model)], axis=0)
    xb = x.astype(BF16)
    slopes = jnp.exp2(-8.0 * (jnp.arange(n_heads, dtype=F32) + 1.0) / n_heads)

    w_dn_all = jnp.pad(ffn_w_out.astype(BF16), ((0, 0), (0, 0), (0, f_pad - d_ff), (0, 0)))

    def ffn(i, s, x, xb):
        hdn = _swiglu_in(xb, ffn_w_in, w_lead=(i, s), f_pad=f_pad, bm=1024, bn=256)
        y = _matmul_res(hdn, w_dn_all, x, w_lead=(i, s), bm=1024, bn=1024, bk=f_pad // 4, alpha=alpha, scale=0.5,
                        vmem_mib=48)
        return _layer_norm(y, ln_g[i, 2 * s], ln_b[i, 2 * s])

    ks, vs, s5r, s5i, lruh, convs = [], [], [], [], [], []
    for i in range(depth):
        x, xb = ffn(i, 0, x, xb)

        proj = _matmul_f32w(xb, w_in, w_lead=(i,), bm=1024, bn=512, out_dtype=F32, vmem_mib=48)

        u_p = proj[:mp, :s5_w].reshape(bp, kp, L, groups, grp_ch)[:, :, ::-1].transpose(3, 1, 0, 2, 4).reshape(groups, kp * bp, L * grp_ch)
        u_s = proj[mp:, :s5_w].reshape(bs, L, groups, grp_ch)[:, ::-1].transpose(2, 0, 1, 3).reshape(groups, bs, L * grp_ch)
        u_g = jnp.concatenate([u_p, u_s], axis=1)
        h0_g = jnp.concatenate([state_s5_re[i], state_s5_im[i]], axis=-1).transpose(1, 0, 2)
        mats = _s5_matrices(s5_lam_re[i], s5_lam_im[i], s5_log_step[i], s5_b_re[i], s5_b_im[i], s5_c_re[i], s5_c_im[i])
        y_g, hfin = _s5_scan(u_g, h0_g, mats, n_seq=bp, n_chunk=kp, n_single=bs)
        y_p = y_g[:, :kp * bp].reshape(groups, kp, bp, L, grp_ch).transpose(2, 1, 3, 0, 4).reshape(mp, s5_w)
        y_s = y_g[:, kp * bp:].reshape(groups, bs, L, grp_ch).transpose(1, 2, 0, 3).reshape(ms, s5_w)
        y_a = _s5_glu(y_p, y_s, proj, s5_d[i], s5_w_glu[i].astype(BF16), s5_b_glu[i])
        hfin = hfin.transpose(1, 0, 2)
        s5r.append((hfin[:bp, :, :p_state], hfin[bp:, :, :p_state]))
        s5i.append((hfin[:bp, :, p_state:], hfin[bp:, :, p_state:]))

        lam_init = 0.8 - 0.6 * math.exp(-0.3 * i)
        dl = diff_lambda[i].astype(F32)
        lam = jnp.exp(jnp.sum(dl[0] * dl[1])) - jnp.exp(jnp.sum(dl[2] * dl[3])) + lam_init
        scal = jnp.concatenate([slopes, lam.reshape(1)])
        y_b = _attn_prompt(proj, slopes, lam, diff_subln[i], n_batch=bp, seq=tp, n_heads=n_heads, hd=hd,
                           out_rows=m, out_scale=1.0 - lam_init, q_col=c_q // e, k_col=c_k // e, v_col=c_v // e)
        y_b = _attn_sample(proj, cache_k, cache_v, scal, diff_subln[i], y_b, layer=i, n_batch=bs, seq=ts,
                           n_heads=n_heads, hd=hd, row0=mp, out_scale=1.0 - lam_init,
                           q_off=c_q, k_off=c_k, v_off=c_v)

        h0_all = jnp.concatenate([jnp.zeros((bp, lru_w), F32), state_lru[i]], axis=0).reshape(bp + bs, 1, lru_w)
        c0_all = jnp.concatenate([jnp.zeros((bp,) + state_conv.shape[2:], F32), state_conv[i]], axis=0)
        sp = jax.nn.softplus(-lru_lambda[i].astype(F32))
        y_c, h_all, c_all = _rglru(proj, h0_all, c0_all, lru_conv_w[i], lru_conv_b[i],
                                   _block_diag(lru_w_a[i]).astype(BF16), lru_b_a[i].reshape(-1),
                                   _block_diag(lru_w_x[i]).astype(BF16), lru_b_x[i].reshape(-1), sp,
                                   m=m, width=lru_w, x_col=c_xr // lru_w, gate_col=c_gr // lru_w,
                                   n_chain_seq=bp, blocks_per_seq=tp // LRU_ROWS)
        lruh.append((h_all[:bp, 0], h_all[bp:, 0]))
        convs.append((c_all[:bp], c_all[bp:]))

        merged = _merge(y_a, y_b, y_c, w_br_a[i].astype(BF16), w_br_b[i].astype(BF16), w_br_c[i].astype(BF16),
                        proj, b_gate[i], gate_col=c_gl // 1024)
        y = _matmul_f32w(merged, w_o, x, w_lead=(i,), bm=1024, bn=512, out_dtype=F32, vmem_mib=48, alpha=alpha)
        x, xb = _layer_norm(y, ln_g[i, 1], ln_b[i, 1])

        x, xb = ffn(i, 1, x, xb)

        pb = jnp.concatenate([p_prompt[i].reshape(mp, -1), p_sample[i].reshape(ms, -1)], axis=0).astype(BF16)
        x, xb = _ple(x, xb, pb, ple_w_gate[i].astype(BF16), ple_w_proj[i].astype(BF16))

        ks.append((proj[:mp, c_k:c_v].reshape(bp, tp, n_heads, e), proj[mp:, c_k:c_v].reshape(bs, ts, n_heads, e)))
        vs.append((proj[:mp, c_v:c_xr].reshape(bp, tp, n_heads, e), proj[mp:, c_v:c_xr].reshape(bs, ts, n_heads, e)))

    stack = lambda pairs, which: jnp.stack([pr[which] for pr in pairs])
    return (x[:mp].reshape(bp, tp, d_model), x[mp:].reshape(bs, ts, d_model),
            stack(ks, 0), stack(vs, 0), stack(s5r, 0), stack(s5i, 0), stack(lruh, 0), stack(convs, 0),
            stack(ks, 1), stack(vs, 1), stack(s5r, 1), stack(s5i, 1), stack(lruh, 1), stack(convs, 1))
```

```python
import functools
import math

import jax
import jax.numpy as jnp
from jax import lax
from jax.experimental import pallas as pl
from jax.experimental.pallas import tpu as pltpu

F32 = jnp.float32
BF16 = jnp.bfloat16

CHUNK = 64
LRU_C = 8.0
LN_EPS = 1e-5
NEG_INF = -1e30
S5_CHUNK = 64
LRU_ROWS = 64

LANES = 128
LOG2E = math.log2(math.e)

MIB = 1024 * 1024


def _params(semantics, vmem_mib):
    return pltpu.CompilerParams(dimension_semantics=semantics, vmem_limit_bytes=vmem_mib * MIB)


def _dot(a, b):
    return jnp.dot(a, b, preferred_element_type=F32)


def _dot_nt(a, b):
    return lax.dot_general(a, b, (((1,), (1,)), ((), ())), preferred_element_type=F32)


def _split_bf16(a):
    hi = a.astype(BF16)
    lo = (a - hi.astype(F32)).astype(BF16)
    return hi, lo


def _dot3(a, b):
    a_hi, a_lo = _split_bf16(a)
    b_hi, b_lo = _split_bf16(b)
    return _dot(a_hi, b_hi) + (_dot(a_lo, b_hi) + _dot(a_hi, b_lo))


def _mm_res_kernel(x_ref, w_ref, r_ref, o_ref, *, nk, alpha, scale):
    part = _dot(x_ref[...], w_ref[...])
    k = pl.program_id(2)

    @pl.when(k == 0)
    def _():
        o_ref[...] = part

    @pl.when(k > 0)
    def _():
        o_ref[...] += part

    @pl.when(k == nk - 1)
    def _():
        o_ref[...] = alpha * r_ref[...] + scale * o_ref[...]


def _matmul_res(x, w, res, *, w_lead=(), bm, bn, bk, alpha, scale, vmem_mib):
    m, kdim = x.shape
    n = w.shape[-1]
    nk = kdim // bk
    assert w.shape[-2] == kdim and m % bm == 0 and n % bn == 0 and kdim % bk == 0 and nk > 1
    tile = pl.BlockSpec((bm, bn), lambda i, j, k: (i, j))
    return pl.pallas_call(
        functools.partial(_mm_res_kernel, nk=nk, alpha=alpha, scale=scale),
        grid=(m // bm, n // bn, nk),
        in_specs=[pl.BlockSpec((bm, bk), lambda i, j, k: (i, k)),
                  pl.BlockSpec((None,) * len(w_lead) + (bk, bn), lambda i, j, k: w_lead + (k, j)), tile],
        out_specs=tile,
        out_shape=jax.ShapeDtypeStruct((m, n), F32),
        compiler_params=_params(("parallel", "parallel", "arbitrary"), vmem_mib),
    )(x, w, res)


def _mm_f32w_kernel(x_ref, w_ref, *rest, alpha, scale):
    o_ref, w_sc = rest[-2:]

    @pl.when(pl.program_id(1) == 0)
    def _():
        w_sc[...] = w_ref[...].astype(BF16)

    acc = _dot(x_ref[...], w_sc[...])
    if len(rest) == 3:
        acc = alpha * rest[0][...] + scale * acc
    o_ref[...] = acc.astype(o_ref.dtype)


def _matmul_f32w(x, w, res=None, *, w_lead=(), skip_cols=(0, 0), bm, bn, out_dtype, vmem_mib, alpha=1.0, scale=1.0):
    m, kdim = x.shape
    lo, hi = skip_cols[0] // bn, skip_cols[1] // bn
    n = w.shape[-1] - (hi - lo) * bn
    assert w.shape[-2] == kdim and m % bm == 0 and n % bn == 0 and skip_cols == (lo * bn, hi * bn)
    w_col = lambda j: j + (hi - lo) * (j >= lo).astype(jnp.int32) if hi > lo else j
    tile = pl.BlockSpec((bm, bn), lambda j, i: (i, j))
    extra = [] if res is None else [res]
    return pl.pallas_call(
        functools.partial(_mm_f32w_kernel, alpha=alpha, scale=scale),
        grid=(n // bn, m // bm),
        in_specs=[pl.BlockSpec((bm, kdim), lambda j, i: (i, 0)),
                  pl.BlockSpec((None,) * len(w_lead) + (kdim, bn), lambda j, i: w_lead + (0, w_col(j)))]
                 + [tile] * len(extra),
        out_specs=tile,
        out_shape=jax.ShapeDtypeStruct((m, n), out_dtype),
        scratch_shapes=[pltpu.VMEM((kdim, bn), BF16)],
        compiler_params=_params(("parallel", "arbitrary"), vmem_mib),
    )(x, w, *extra)


def _mm_f32w_slab_kernel(x_ref, w_ref, _, o_ref, w_sc):
    @pl.when(pl.program_id(1) == 0)
    def _():
        w_sc[...] = w_ref[...].astype(BF16)

    o_ref[...] = _dot(x_ref[...], w_sc[...])


def _matmul_f32w_slab(x, w, buf, *, w_lead, w_col0, row0, slab, bm, bn, vmem_mib):
    kdim = x.shape[1]
    _, rows, cols = buf.shape
    assert rows % bm == 0 and cols % bn == 0 and row0 % bm == 0 and w_col0 % bn == 0 and buf.dtype == F32
    return pl.pallas_call(
        _mm_f32w_slab_kernel,
        grid=(cols // bn, rows // bm),
        in_specs=[pl.BlockSpec((bm, kdim), lambda j, i: (row0 // bm + i, 0)),
                  pl.BlockSpec((None,) * len(w_lead) + (kdim, bn), lambda j, i: w_lead + (0, w_col0 // bn + j)),
                  pl.BlockSpec(memory_space=pl.ANY)],
        out_specs=pl.BlockSpec((None, bm, bn), lambda j, i: (slab, i, j)),
        out_shape=jax.ShapeDtypeStruct(buf.shape, F32),
        input_output_aliases={2: 0},
        scratch_shapes=[pltpu.VMEM((kdim, bn), BF16)],
        compiler_params=_params(("parallel", "arbitrary"), vmem_mib),
    )(x, w, buf)


def _swiglu_kernel(x_ref, wg_ref, wu_ref, o_ref, wg_sc, wu_sc, *, n_real):
    j = pl.program_id(0)

    @pl.when(jnp.logical_and(pl.program_id(1) == 0, j < n_real))
    def _():
        wg_sc[...] = wg_ref[...].astype(BF16)
        wu_sc[...] = wu_ref[...].astype(BF16)

    @pl.when(j < n_real)
    def _():
        x = x_ref[...]
        g = _dot(x, wg_sc[...])
        u = _dot(x, wu_sc[...])
        o_ref[...] = (g * jax.nn.sigmoid(g) * u).astype(o_ref.dtype)

    @pl.when(j >= n_real)
    def _():
        o_ref[...] = jnp.zeros(o_ref.shape, o_ref.dtype)


def _swiglu_in(xb, w_gu, *, w_lead, f_pad, bm, bn):
    m, kdim = xb.shape
    f = w_gu.shape[-1] // 2
    n_real = f // bn
    nj = f_pad // bn
    assert f % bn == 0 and f_pad % bn == 0 and m % bm == 0
    lead = (None,) * len(w_lead)
    col = lambda j: jnp.minimum(j, n_real - 1)
    return pl.pallas_call(
        functools.partial(_swiglu_kernel, n_real=n_real),
        grid=(nj, m // bm),
        in_specs=[pl.BlockSpec((bm, kdim), lambda j, i: (i, 0)),
                  pl.BlockSpec(lead + (kdim, bn), lambda j, i: w_lead + (0, col(j))),
                  pl.BlockSpec(lead + (kdim, bn), lambda j, i: w_lead + (0, col(j) + n_real))],
        out_specs=pl.BlockSpec((bm, bn), lambda j, i: (i, j)),
        out_shape=jax.ShapeDtypeStruct((m, f_pad), BF16),
        scratch_shapes=[pltpu.VMEM((kdim, bn), BF16), pltpu.VMEM((kdim, bn), BF16)],
        compiler_params=_params(("parallel", "arbitrary"), 48),
    )(xb, w_gu, w_gu)


def _ln_kernel(y_ref, g_ref, b_ref, o_ref, ob_ref):
    y = y_ref[...]
    mu = jnp.mean(y, axis=-1, keepdims=True)
    d = y - mu
    var = jnp.mean(d * d, axis=-1, keepdims=True)
    out = d * lax.rsqrt(var + LN_EPS) * g_ref[...] + b_ref[...]
    o_ref[...] = out
    ob_ref[...] = out.astype(BF16)


def _layer_norm(y, g, b, *, bm=256):
    m, d = y.shape
    row = pl.BlockSpec((bm, d), lambda i: (i, 0))
    vec = pl.BlockSpec((1, d), lambda i: (0, 0))
    return pl.pallas_call(
        _ln_kernel,
        grid=(m // bm,),
        in_specs=[row, vec, vec],
        out_specs=[row, row],
        out_shape=[jax.ShapeDtypeStruct((m, d), F32), jax.ShapeDtypeStruct((m, d), BF16)],
        compiler_params=_params(("parallel",), 40),
    )(y, g.reshape(1, d), b.reshape(1, d))


def _s5_kernel(u_ref, h0_ref, min_ref, minsw_ref, kseq_ref, mout_ref, dec_ref, y_ref, hfin_ref,
               s_sc, ssw_sc, hprev_sc, toep_sc, *, n_seq, n_chunk, n_single):
    ch, seq_w = kseq_ref.shape
    width = toep_sc.shape[1]
    per_tile = LANES // ch
    kseq = kseq_ref[...]
    for b in range(per_tile):
        rot = kseq if b == 0 else pltpu.roll(kseq, seq_w - ch * b, axis=1)
        for a in range(width // LANES):
            sigma = width // ch - 1 - (a * per_tile + b)
            toep_sc[sigma * ch:(sigma + 1) * ch, :] = rot[:, a * LANES:a * LANES + width]

    u = u_ref[...]
    u_hi, u_lo = _split_bf16(u)

    def dot3_u(w):
        w_hi, w_lo = _split_bf16(w)
        return _dot(u_hi, w_hi) + (_dot(u_lo, w_hi) + _dot(u_hi, w_lo))

    s_sc[...] = dot3_u(min_ref[...])
    ssw_sc[...] = dot3_u(minsw_ref[...])
    a1 = dec_ref[0:1, :]
    a2 = dec_ref[1:2, :]
    a2sw = dec_ref[2:3, :]

    h = jnp.zeros((n_seq, s_sc.shape[1]), F32)
    hsw = h
    for k in range(n_chunk):
        rows = slice(k * n_seq, (k + 1) * n_seq)
        hprev_sc[rows, :] = h
        h, hsw = (a1 * h + a2 * hsw + s_sc[rows, :], a1 * hsw + a2sw * h + ssw_sc[rows, :])
    n_chain = n_chunk * n_seq
    hfin_ref[0:n_seq, :] = h
    h0 = h0_ref[...]
    h0sw = pltpu.roll(h0, h0.shape[1] // 2, axis=1)
    hprev_sc[n_chain:n_chain + n_single, :] = h0
    hfin_ref[n_seq:n_seq + n_single, :] = a1 * h0 + a2 * h0sw + s_sc[n_chain:n_chain + n_single, :]

    y_ref[...] = dot3_u(toep_sc[...]) + _dot3(hprev_sc[...], mout_ref[...])


def _s5_scan(u_g, h0_g, mats, *, n_seq, n_chunk, n_single):
    m_in, m_in_sw, kseq, m_out, dec = mats
    g, rows, width = u_g.shape
    p2 = m_in.shape[2]
    n_out = n_seq + n_single
    grp = lambda *shape: pl.BlockSpec((None,) + shape, lambda i: (i,) + (0,) * len(shape))
    return pl.pallas_call(
        functools.partial(_s5_kernel, n_seq=n_seq, n_chunk=n_chunk, n_single=n_single),
        grid=(g,),
        in_specs=[grp(rows, width), grp(n_single, p2), grp(width, p2), grp(width, p2),
                  grp(*kseq.shape[1:]), grp(p2, width), grp(3, p2)],
        out_specs=[grp(rows, width), grp(n_out, p2)],
        out_shape=[jax.ShapeDtypeStruct((g, rows, width), F32),
                   jax.ShapeDtypeStruct((g, n_out, p2), F32)],
        scratch_shapes=[pltpu.VMEM((rows, p2), F32), pltpu.VMEM((rows, p2), F32),
                        pltpu.VMEM((rows, p2), F32), pltpu.VMEM((width, width), F32)],
        compiler_params=_params(("parallel",), 40),
    )(u_g, h0_g, m_in, m_in_sw, kseq, m_out, dec)


def _s5_matrices(lam_re, lam_im, log_step, b_re, b_im, c_re, c_im):
    hp = lax.Precision.HIGHEST
    L = S5_CHUNK
    g, p = lam_re.shape
    ch = b_re.shape[2]
    lam = lax.complex(lam_re.astype(F32), lam_im.astype(F32))
    step = jnp.exp(log_step.astype(F32))[:, None]
    lam_step = lam * step
    lam_bar = jnp.exp(lam_step)
    b_bar = ((lam_bar - 1.0) / lam)[..., None] * lax.complex(b_re.astype(F32), b_im.astype(F32))
    c_mat = lax.complex(c_re.astype(F32), c_im.astype(F32))
    d = jnp.arange(L + 1, dtype=F32)
    pw = jnp.exp(lam_step[None] * d[:, None, None])
    w_in = pw[:L][::-1].transpose(1, 0, 2)[:, :, None, :] * b_bar.transpose(0, 2, 1)[:, None, :, :]
    w_in = w_in.reshape(g, L * ch, p)
    m_in = jnp.concatenate([jnp.real(w_in), jnp.imag(w_in)], axis=-1)
    m_in_sw = jnp.concatenate([jnp.imag(w_in), jnp.real(w_in)], axis=-1)
    w_out = pw[1:].transpose(1, 2, 0)[:, :, :, None] * c_mat.transpose(0, 2, 1)[:, :, None, :]
    w_out = w_out.reshape(g, p, L * ch)
    m_out = jnp.concatenate([jnp.real(w_out), -jnp.imag(w_out)], axis=1)
    cp = c_mat[None] * pw[:L][:, :, None, :]
    kd = (jnp.einsum('dgcp,gpe->dgce', jnp.real(cp), jnp.real(b_bar), precision=hp)
          - jnp.einsum('dgcp,gpe->dgce', jnp.imag(cp), jnp.imag(b_bar), precision=hp))
    kseq = jnp.pad(kd.transpose(1, 3, 0, 2), ((0, 0), (0, 0), (L - 1, 1), (0, 0))).reshape(g, ch, 2 * L * ch)
    pl_ = pw[L]
    dec = jnp.stack([jnp.concatenate([jnp.real(pl_), jnp.real(pl_)], -1),
                     jnp.concatenate([-jnp.imag(pl_), jnp.imag(pl_)], -1),
                     jnp.concatenate([jnp.imag(pl_), -jnp.imag(pl_)], -1)], axis=1)
    return m_in, m_in_sw, kseq, m_out, dec


def _glu_kernel(y1_ref, y2_ref, u_ref, d_ref, w_ref, b_ref, o_ref, *, n_first):
    def run(y_ref):
        y = jax.nn.gelu(y_ref[...] + d_ref[...] * u_ref[...])
        z = _dot(y.astype(BF16), w_ref[...]) + b_ref[...]
        o_ref[...] = (y * jax.nn.sigmoid(z)).astype(BF16)

    pl.when(pl.program_id(0) < n_first)(lambda: run(y1_ref))
    pl.when(pl.program_id(0) >= n_first)(lambda: run(y2_ref))


def _s5_glu(y1, y2, proj, d_skip, w_glu, b_glu, *, bm=512):
    (m1, w), m2 = y1.shape, y2.shape[0]
    assert m1 % bm == 0 and m2 % bm == 0
    n_first = m1 // bm
    row = pl.BlockSpec((bm, w), lambda i: (i, 0))
    vec = pl.BlockSpec((1, w), lambda i: (0, 0))
    return pl.pallas_call(
        functools.partial(_glu_kernel, n_first=n_first),
        grid=((m1 + m2) // bm,),
        in_specs=[pl.BlockSpec((bm, w), lambda i: (jnp.minimum(i, n_first - 1), 0)),
                  pl.BlockSpec((bm, w), lambda i: (jnp.maximum(i - n_first, 0), 0)),
                  row, vec, pl.BlockSpec((w, w), lambda i: (0, 0)), vec],
        out_specs=row,
        out_shape=jax.ShapeDtypeStruct((m1 + m2, w), BF16),
        compiler_params=_params(("parallel",), 32),
    )(y1, y2, proj, d_skip.reshape(1, w), w_glu, b_glu.reshape(1, w))


def _lru_kernel(x_ref, gate_ref, h0_ref, c0_ref, cw_ref, cb_ref, wa_ref, ba_ref, wx_ref, bx_ref, sp_ref,
                y_ref, hout_ref, cout_ref, xpad_sc, a_sc, b_sc, hs_sc, h_sc,
                *, rows, n_chain_blocks, blocks_per_seq, conv_width):
    blk = pl.program_id(0)
    tail = conv_width - 1
    base = 8
    is_start = jnp.logical_or(blk >= n_chain_blocks, blk % blocks_per_seq == 0)

    @pl.when(is_start)
    def _():
        h_sc[...] = h0_ref[...]
        xpad_sc[base - tail:base, :] = c0_ref[...]

    x = x_ref[...]
    xpad_sc[base:base + rows, :] = x
    xc = cb_ref[...]
    for j in range(conv_width):
        off = base - tail + j
        xc = xc + cw_ref[j:j + 1, :] * xpad_sc[off:off + rows, :]
    new_tail = x_ref[rows - tail:rows, :]
    xpad_sc[base - tail:base, :] = new_tail
    cout_ref[...] = new_tail

    xcb = xc.astype(BF16)
    r = jax.nn.sigmoid(_dot(xcb, wa_ref[...]) + ba_ref[...])
    ig = jax.nn.sigmoid(_dot(xcb, wx_ref[...]) + bx_ref[...])
    log_a = -LRU_C * r * sp_ref[...]
    a = jnp.exp(log_a)
    mult = jnp.sqrt(-jnp.tanh(log_a) * (a * a + 1.0))
    a_sc[...] = a
    b_sc[...] = mult * (ig * xc)

    def step(t, h):
        h = a_sc[pl.ds(t, 1), :] * h + b_sc[pl.ds(t, 1), :]
        hs_sc[pl.ds(t, 1), :] = h
        return h

    h = lax.fori_loop(0, rows, step, h_sc[...], unroll=8)
    h_sc[...] = h
    hout_ref[...] = h
    y_ref[...] = (hs_sc[...] * jax.nn.gelu(gate_ref[...])).astype(BF16)


def _rglru(proj, h0_all, c0_all, conv_w, conv_b, wa_bd, b_a, wx_bd, b_x, sp, *, m, width, x_col, gate_col,
           n_chain_seq, blocks_per_seq):
    rows = LRU_ROWS
    nblk = m // rows
    n_chain_blocks = n_chain_seq * blocks_per_seq
    n_seq = h0_all.shape[0]
    cw = conv_w.shape[0]

    def seq_of(i):
        return jnp.where(i < n_chain_blocks, i // blocks_per_seq, n_chain_seq + i - n_chain_blocks)

    vec = pl.BlockSpec((1, width), lambda i: (0, 0))
    mat = pl.BlockSpec((width, width), lambda i: (0, 0))
    hspec = pl.BlockSpec((None, 1, width), lambda i: (seq_of(i), 0, 0))
    cspec = pl.BlockSpec((None, cw - 1, width), lambda i: (seq_of(i), 0, 0))
    return pl.pallas_call(
        functools.partial(_lru_kernel, rows=rows, n_chain_blocks=n_chain_blocks,
                          blocks_per_seq=blocks_per_seq, conv_width=cw),
        grid=(nblk,),
        in_specs=[pl.BlockSpec((rows, width), lambda i: (i, x_col)),
                  pl.BlockSpec((rows, width), lambda i: (i, gate_col)),
                  hspec, cspec, pl.BlockSpec((cw, width), lambda i: (0, 0)), vec, mat, vec, mat, vec, vec],
        out_specs=[pl.BlockSpec((rows, width), lambda i: (i, 0)), hspec, cspec],
        out_shape=[jax.ShapeDtypeStruct((m, width), BF16),
                   jax.ShapeDtypeStruct((n_seq, 1, width), F32),
                   jax.ShapeDtypeStruct((n_seq, cw - 1, width), F32)],
        scratch_shapes=[pltpu.VMEM((rows + 8, width), F32), pltpu.VMEM((rows, width), F32),
                        pltpu.VMEM((rows, width), F32), pltpu.VMEM((rows, width), F32),
                        pltpu.VMEM((1, width), F32)],
        compiler_params=_params(("arbitrary",), 32),
    )(proj, proj, h0_all, c0_all, conv_w, conv_b.reshape(1, width), wa_bd, b_a.reshape(1, width),
      wx_bd, b_x.reshape(1, width), sp.reshape(1, width))


def _block_diag(w):
    n, c, d = w.shape
    eye = jnp.eye(n, dtype=w.dtype)
    return (eye[:, None, :, None] * w[:, :, None, :]).reshape(n * c, n * d)


def _head_norm(o, g, out_scale):
    o = o * lax.rsqrt(jnp.mean(o * o, axis=-1, keepdims=True) + LN_EPS) * g
    return o * out_scale


def _attn_prompt_kernel(scal_ref, sbt_ref, q_ref, k_ref, v_ref, g_ref, _, o_ref, m_sc, l_sc, acc_sc, doff_sc, ddiag_sc,
                        *, blk, hd, n_heads, out_scale):
    h = pl.program_id(1)
    i = pl.program_id(2)
    nq = pl.num_programs(2)
    slope2 = scal_ref[h]
    lam = scal_ref[n_heads]
    c1 = (hd ** -0.5) * LOG2E
    shift = int(math.log2(CHUNK))
    e = 2 * hd

    @pl.when(i == 0)
    def _():
        r = lax.broadcasted_iota(jnp.int32, (blk, blk), 0)
        c = lax.broadcasted_iota(jnp.int32, (blk, blk), 1)
        rel = r - c
        doff_sc[...] = slope2 * rel.astype(F32)
        visible = lax.shift_right_arithmetic(c, shift) <= lax.shift_right_arithmetic(r, shift)
        ddiag_sc[...] = jnp.where(visible, slope2 * jnp.abs(rel).astype(F32), -NEG_INF)

    q = q_ref[...].astype(BF16)
    m_sc[...] = jnp.full(m_sc.shape, NEG_INF, F32)
    l_sc[...] = jnp.zeros(l_sc.shape, F32)
    acc_sc[...] = jnp.zeros(acc_sc.shape, F32)

    def process(j, d_ref, sb):
        start = pl.multiple_of(j * blk, blk)
        kj = k_ref[pl.ds(start, blk), :].astype(BF16)
        vj = v_ref[pl.ds(start, blk), :].astype(BF16)
        for c in range(2):
            cols = slice(c * hd, (c + 1) * hd)
            t = _dot_nt(q[:, cols], kj[:, cols]) * c1 - d_ref[...]
            m_prev = m_sc[c]
            m_next = jnp.maximum(m_prev, jnp.max(t, axis=1, keepdims=True) - sb)
            p = jnp.exp2(t - jnp.tile(m_next + sb, (1, blk // LANES)))
            corr = jnp.exp2(m_prev - m_next)
            psum = p[:, 0:LANES]
            for w in range(1, blk // LANES):
                psum = psum + p[:, w * LANES:(w + 1) * LANES]
            l_sc[c] = corr * l_sc[c] + psum
            acc_sc[c] = acc_sc[c] * jnp.tile(corr, (1, e // LANES)) + _dot(p.astype(BF16), vj)
            m_sc[c] = m_next

    def body(j, carry):
        process(j, doff_sc, sbt_ref[h * nq + (i - j)])
        return carry

    lax.fori_loop(0, i, body, 0)
    process(i, ddiag_sc, 0.0)
    outs = [acc_sc[c] / jnp.sum(l_sc[c], axis=1, keepdims=True) for c in range(2)]
    o = outs[0] - lam * outs[1]
    o_ref[...] = _head_norm(o, g_ref[...], out_scale).astype(BF16)


def _attn_prompt(proj, k_all, v_all, slopes, lam, subln, y_buf, *, layer, n_batch, seq, n_heads, hd, q_col,
                 out_scale, blk=512):
    assert blk % CHUNK == 0 and blk % LANES == 0 and seq % blk == 0
    assert y_buf.shape[1] == n_heads * 2 * hd and y_buf.dtype == BF16
    nq = seq // blk
    e = 2 * hd
    scal = jnp.concatenate([slopes * LOG2E, lam.reshape(1)])
    sb_tab = (slopes[:, None] * (LOG2E * blk * jnp.arange(nq, dtype=F32))[None, :]).reshape(-1)
    smem = pl.BlockSpec(memory_space=pltpu.SMEM)
    return pl.pallas_call(
        functools.partial(_attn_prompt_kernel, blk=blk, hd=hd, n_heads=n_heads, out_scale=out_scale),
        grid=(n_batch, n_heads, nq),
        in_specs=[smem, smem,
                  pl.BlockSpec((blk, e), lambda b, h, i: (b * nq + i, q_col + h)),
                  pl.BlockSpec((None, seq, e), lambda b, h, i: (layer, b, h)),
                  pl.BlockSpec((None, seq, e), lambda b, h, i: (layer, b, h)),
                  pl.BlockSpec((1, e), lambda b, h, i: (0, 0)), pl.BlockSpec(memory_space=pl.ANY)],
        out_specs=pl.BlockSpec((blk, e), lambda b, h, i: (b * nq + i, h)),
        out_shape=jax.ShapeDtypeStruct(y_buf.shape, BF16),
        input_output_aliases={6: 0},
        scratch_shapes=[pltpu.VMEM((2, blk, LANES), F32), pltpu.VMEM((2, blk, LANES), F32),
                        pltpu.VMEM((2, blk, e), F32), pltpu.VMEM((blk, blk), F32), pltpu.VMEM((blk, blk), F32)],
        compiler_params=_params(("parallel", "parallel", "arbitrary"), 48),
    )(scal, sb_tab, proj, k_all, v_all, subln.reshape(1, e), y_buf)


def _attn_sample_kernel(scal_ref, q_ref, kn_ref, vn_ref, g_ref, kc_hbm, vc_hbm, _, o_ref, kbuf, vbuf, sem,
                        *, layer, hd, n_heads, past, out_scale):
    b = pl.program_id(0)
    h = pl.program_id(1)
    step = b * n_heads + h
    n_steps = pl.num_programs(0) * n_heads
    slot = lax.rem(step, 2)

    def cache_copies(at_step, at_slot):
        bb = at_step // n_heads
        hh = lax.rem(at_step, n_heads)
        return (pltpu.make_async_copy(kc_hbm.at[layer, bb, :, hh, :], kbuf.at[at_slot], sem.at[0, at_slot]),
                pltpu.make_async_copy(vc_hbm.at[layer, bb, :, hh, :], vbuf.at[at_slot], sem.at[1, at_slot]))

    @pl.when(step == 0)
    def _():
        for cp in cache_copies(step, slot):
            cp.start()

    @pl.when(step + 1 < n_steps)
    def _():
        for cp in cache_copies(step + 1, 1 - slot):
            cp.start()

    slope = scal_ref[h]
    lam = scal_ref[n_heads]
    scale = hd ** -0.5
    shift = int(math.log2(CHUNK))
    t = q_ref.shape[0]
    q = q_ref[...].astype(BF16)
    kn = kn_ref[...].astype(BF16)
    vn = vn_ref[...].astype(BF16)
    qpos = past + lax.broadcasted_iota(jnp.int32, (t, 1), 0)
    bias_c = slope * (qpos - lax.broadcasted_iota(jnp.int32, (1, past), 1)).astype(F32)
    kposn = past + lax.broadcasted_iota(jnp.int32, (1, t), 1)
    bias_n = slope * jnp.abs(qpos - kposn).astype(F32)
    vis_n = lax.shift_right_arithmetic(kposn, shift) <= lax.shift_right_arithmetic(qpos, shift)

    for cp in cache_copies(step, slot):
        cp.wait()
    kc = kbuf[slot].astype(BF16)
    vc = vbuf[slot].astype(BF16)
    outs = []
    for c in range(2):
        cols = slice(c * hd, (c + 1) * hd)
        s_c = _dot_nt(q[:, cols], kc[:, cols]) * scale - bias_c
        s_n = jnp.where(vis_n, _dot_nt(q[:, cols], kn[:, cols]) * scale - bias_n, NEG_INF)
        m = jnp.maximum(jnp.max(s_c, axis=-1, keepdims=True), jnp.max(s_n, axis=-1, keepdims=True))
        p_c = jnp.exp(s_c - m)
        p_n = jnp.exp(s_n - m)
        l = jnp.sum(p_c, axis=-1, keepdims=True) + jnp.sum(p_n, axis=-1, keepdims=True)
        outs.append((_dot(p_c.astype(BF16), vc) + _dot(p_n.astype(BF16), vn)) / l)
    o = outs[0] - lam * outs[1]
    o_ref[...] = _head_norm(o, g_ref[...], out_scale).astype(BF16)


def _attn_sample(proj, k_new, v_new, cache_k, cache_v, scal, subln, y_buf, *, layer, n_batch, seq, n_heads, hd,
                 q_col, row0, out_scale):
    e = 2 * hd
    past = cache_k.shape[2]
    assert y_buf.shape[1] == n_heads * e and y_buf.dtype == BF16 and row0 % seq == 0
    rb0 = row0 // seq
    new = pl.BlockSpec((None, seq, e), lambda b, h: (layer, b, h))
    hbm = pl.BlockSpec(memory_space=pl.ANY)
    return pl.pallas_call(
        functools.partial(_attn_sample_kernel, layer=layer, hd=hd, n_heads=n_heads, past=past, out_scale=out_scale),
        grid=(n_batch, n_heads),
        in_specs=[pl.BlockSpec(memory_space=pltpu.SMEM), pl.BlockSpec((seq, e), lambda b, h: (rb0 + b, q_col + h)),
                  new, new, pl.BlockSpec((1, e), lambda b, h: (0, 0)), hbm, hbm, hbm],
        out_specs=pl.BlockSpec((seq, e), lambda b, h: (rb0 + b, h)),
        out_shape=jax.ShapeDtypeStruct(y_buf.shape, BF16),
        input_output_aliases={7: 0},
        scratch_shapes=[pltpu.VMEM((2, past, e), F32), pltpu.VMEM((2, past, e), F32),
                        pltpu.SemaphoreType.DMA((2, 2))],
        compiler_params=_params(("arbitrary", "arbitrary"), 40),
    )(scal, proj, k_new, v_new, subln.reshape(1, e), cache_k, cache_v, y_buf)


def _merge_kernel(ya_ref, yb_ref, yc_ref, wa_ref, wb_ref, wc_ref, ga_ref, gb_ref, gc_ref, bg_ref, o_ref):
    def branch(y_ref, w_ref, g_ref, r):
        return jax.nn.sigmoid(g_ref[...] + bg_ref[r:r + 1, :]) * _dot(y_ref[...], w_ref[...])

    merged = branch(ya_ref, wa_ref, ga_ref, 0) + branch(yb_ref, wb_ref, gb_ref, 1) + branch(yc_ref, wc_ref, gc_ref, 2)
    o_ref[...] = merged.astype(BF16)


def _merge(ya, yb, yc, wa, wb, wc, proj, b_gate, *, gate_col, bm=512, bn=1024):
    m = ya.shape[0]
    d = wa.shape[1]
    nj = d // bn
    y_spec = lambda y: pl.BlockSpec((bm, y.shape[1]), lambda j, i: (i, 0))
    w_spec = lambda w: pl.BlockSpec((w.shape[0], bn), lambda j, i: (0, j))
    g_spec = lambda r: pl.BlockSpec((bm, bn), lambda j, i: (i, gate_col + r * nj + j))
    return pl.pallas_call(
        _merge_kernel,
        grid=(nj, m // bm),
        in_specs=[y_spec(ya), y_spec(yb), y_spec(yc), w_spec(wa), w_spec(wb), w_spec(wc),
                  g_spec(0), g_spec(1), g_spec(2), pl.BlockSpec((3, bn), lambda j, i: (0, j))],
        out_specs=pl.BlockSpec((bm, bn), lambda j, i: (i, j)),
        out_shape=jax.ShapeDtypeStruct((m, d), BF16),
        compiler_params=_params(("parallel", "parallel"), 48),
    )(ya, yb, yc, wa, wb, wc, proj, proj, proj, b_gate)


def _ple_kernel(xb_ref, wg_ref, p_ref, wp_ref, x_ref, o_ref, ob_ref):
    gate = jax.nn.sigmoid(_dot(xb_ref[...], wg_ref[...]))
    out = x_ref[...] + gate * _dot(p_ref[...], wp_ref[...])
    o_ref[...] = out
    ob_ref[...] = out.astype(BF16)


def _ple(x, xb, pb, w_gate, w_proj, *, bm=512, bn=1024):
    m, d = x.shape
    pdim = pb.shape[1]
    tile = pl.BlockSpec((bm, bn), lambda j, i: (i, j))
    return pl.pallas_call(
        _ple_kernel,
        grid=(d // bn, m // bm),
        in_specs=[pl.BlockSpec((bm, d), lambda j, i: (i, 0)), pl.BlockSpec((d, bn), lambda j, i: (0, j)),
                  pl.BlockSpec((bm, pdim), lambda j, i: (i, 0)), pl.BlockSpec((pdim, bn), lambda j, i: (0, j)),
                  tile],
        out_specs=[tile, tile],
        out_shape=[jax.ShapeDtypeStruct((m, d), F32), jax.ShapeDtypeStruct((m, d), BF16)],
        compiler_params=_params(("parallel", "parallel"), 48),
    )(xb, w_gate, pb, w_proj, x)


def kernel(x_prompt, x_sample, cache_k, cache_v, state_s5_re, state_s5_im, state_lru, state_conv, p_prompt, p_sample, ln_g, ln_b, ffn_w_in, ffn_w_out, w_in, b_gate, s5_lam_re, s5_lam_im, s5_log_step, s5_b_re, s5_b_im, s5_c_re, s5_c_im, s5_d, s5_w_glu, s5_b_glu, diff_lambda, diff_subln, lru_conv_w, lru_conv_b, lru_w_a, lru_b_a, lru_w_x, lru_b_x, lru_lambda, w_br_a, w_br_b, w_br_c, w_o, ple_w_proj, ple_w_gate):
    bp, tp, d_model = x_prompt.shape
    bs, ts, _ = x_sample.shape
    depth = ln_g.shape[0]
    mp, ms = bp * tp, bs * ts
    m = mp + ms
    n_heads, e = cache_k.shape[3], cache_k.shape[4]
    hd = e // 2
    past = cache_k.shape[2]
    groups, p_state, grp_ch = s5_b_re.shape[1:]
    s5_w = groups * grp_ch
    att_w = n_heads * e
    lru_w = lru_lambda.shape[1]
    d_ff = ffn_w_out.shape[2]
    f_pad = -(-d_ff // 1024) * 1024
    alpha = (2 * depth) ** 0.25
    L = S5_CHUNK
    assert ts == L and tp % L == 0 and ts == LRU_ROWS and ts == CHUNK and past % CHUNK == 0
    kp = tp // L
    w_k, w_v = s5_w + att_w, s5_w + 2 * att_w
    c_q = s5_w
    c_xr = s5_w + att_w
    c_gr = c_xr + lru_w
    c_gl = c_gr + lru_w
    kv_bufs = [jnp.zeros((depth, rows, att_w), F32) for rows in (mp, mp, ms, ms)]

    x = jnp.concatenate([x_prompt.reshape(mp, d_model), x_sample.reshape(ms, d_model)], axis=0)
    xb = x.astype(BF16)
    slopes = jnp.exp2(-8.0 * (jnp.arange(n_heads, dtype=F32) + 1.0) / n_heads)

    w_dn_all = jnp.pad(ffn_w_out.astype(BF16), ((0, 0), (0, 0), (0, f_pad - d_ff), (0, 0)))

    def ffn(i, s, x, xb):
        hdn = _swiglu_in(xb, ffn_w_in, w_lead=(i, s), f_pad=f_pad, bm=1024, bn=256)
        y = _matmul_res(hdn, w_dn_all, x, w_lead=(i, s), bm=1024, bn=1024, bk=f_pad // 4, alpha=alpha, scale=0.5,
                        vmem_mib=48)
        return _layer_norm(y, ln_g[i, 2 * s], ln_b[i, 2 * s])

    s5r, s5i, lruh, convs = [], [], [], []
    for i in range(depth):
        x, xb = ffn(i, 0, x, xb)

        proj = _matmul_f32w(xb, w_in, w_lead=(i,), skip_cols=(w_k, w_v + att_w), bm=1024, bn=512, out_dtype=F32,
                            vmem_mib=48)
        kv_bufs = [_matmul_f32w_slab(xb, w_in, buf, w_lead=(i,), w_col0=col, row0=row0, slab=i, bm=1024, bn=512,
                                     vmem_mib=48)
                   for buf, col, row0 in zip(kv_bufs, (w_k, w_v, w_k, w_v), (0, 0, mp, mp))]
        k_p, v_p, k_s, v_s = kv_bufs

        u_p = proj[:mp, :s5_w].reshape(bp, kp, L, groups, grp_ch).transpose(3, 1, 0, 2, 4).reshape(groups, kp * bp, L * grp_ch)
        u_s = proj[mp:, :s5_w].reshape(bs, L, groups, grp_ch).transpose(2, 0, 1, 3).reshape(groups, bs, L * grp_ch)
        u_g = jnp.concatenate([u_p, u_s], axis=1)
        h0_g = jnp.concatenate([state_s5_re[i], state_s5_im[i]], axis=-1).transpose(1, 0, 2)
        mats = _s5_matrices(s5_lam_re[i], s5_lam_im[i], s5_log_step[i], s5_b_re[i], s5_b_im[i], s5_c_re[i], s5_c_im[i])
        y_g, hfin = _s5_scan(u_g, h0_g, mats, n_seq=bp, n_chunk=kp, n_single=bs)
        y_p = y_g[:, :kp * bp].reshape(groups, kp, bp, L, grp_ch).transpose(2, 1, 3, 0, 4).reshape(mp, s5_w)
        y_s = y_g[:, kp * bp:].reshape(groups, bs, L, grp_ch).transpose(1, 2, 0, 3).reshape(ms, s5_w)
        y_a = _s5_glu(y_p, y_s, proj, s5_d[i], s5_w_glu[i].astype(BF16), s5_b_glu[i])
        hfin = hfin.transpose(1, 0, 2)
        s5r.append((hfin[:bp, :, :p_state], hfin[bp:, :, :p_state]))
        s5i.append((hfin[:bp, :, p_state:], hfin[bp:, :, p_state:]))

        lam_init = 0.8 - 0.6 * math.exp(-0.3 * i)
        dl = diff_lambda[i].astype(F32)
        lam = jnp.exp(jnp.sum(dl[0] * dl[1])) - jnp.exp(jnp.sum(dl[2] * dl[3])) + lam_init
        scal = jnp.concatenate([slopes, lam.reshape(1)])
        y_b = _attn_prompt(proj, k_p, v_p, slopes, lam, diff_subln[i], jnp.zeros((m, att_w), BF16), layer=i,
                           n_batch=bp, seq=tp, n_heads=n_heads, hd=hd, out_scale=1.0 - lam_init, q_col=c_q // e)
        y_b = _attn_sample(proj, k_s, v_s, cache_k, cache_v, scal, diff_subln[i], y_b, layer=i, n_batch=bs, seq=ts,
                           n_heads=n_heads, hd=hd, row0=mp, out_scale=1.0 - lam_init, q_col=c_q // e)

        h0_all = jnp.concatenate([jnp.zeros((bp, lru_w), F32), state_lru[i]], axis=0).reshape(bp + bs, 1, lru_w)
        c0_all = jnp.concatenate([jnp.zeros((bp,) + state_conv.shape[2:], F32), state_conv[i]], axis=0)
        sp = jax.nn.softplus(-lru_lambda[i].astype(F32))
        y_c, h_all, c_all = _rglru(proj, h0_all, c0_all, lru_conv_w[i], lru_conv_b[i],
                                   _block_diag(lru_w_a[i]).astype(BF16), lru_b_a[i].reshape(-1),
                                   _block_diag(lru_w_x[i]).astype(BF16), lru_b_x[i].reshape(-1), sp,
                                   m=m, width=lru_w, x_col=c_xr // lru_w, gate_col=c_gr // lru_w,
                                   n_chain_seq=bp, blocks_per_seq=tp // LRU_ROWS)
        lruh.append((h_all[:bp, 0], h_all[bp:, 0]))
        convs.append((c_all[:bp], c_all[bp:]))

        merged = _merge(y_a, y_b, y_c, w_br_a[i].astype(BF16), w_br_b[i].astype(BF16), w_br_c[i].astype(BF16),
                        proj, b_gate[i], gate_col=c_gl // 1024)
        y = _matmul_f32w(merged, w_o, x, w_lead=(i,), bm=1024, bn=512, out_dtype=F32, vmem_mib=48, alpha=alpha)
        x, xb = _layer_norm(y, ln_g[i, 1], ln_b[i, 1])

        x, xb = ffn(i, 1, x, xb)

        pb = jnp.concatenate([p_prompt[i].reshape(mp, -1), p_sample[i].reshape(ms, -1)], axis=0).astype(BF16)
        x, xb = _ple(x, xb, pb, ple_w_gate[i].astype(BF16), ple_w_proj[i].astype(BF16))


    stack = lambda pairs, which: jnp.stack([pr[which] for pr in pairs])
    return (x[:mp].reshape(bp, tp, d_model), x[mp:].reshape(bs, ts, d_model),
            k_p.reshape(depth, bp, tp, n_heads, e), v_p.reshape(depth, bp, tp, n_heads, e),
            stack(s5r, 0), stack(s5i, 0), stack(lruh, 0), stack(convs, 0),
            k_s.reshape(depth, bs, ts, n_heads, e), v_s.reshape(depth, bs, ts, n_heads, e),
            stack(s5r, 1), stack(s5i, 1), stack(lruh, 1), stack(convs, 1))
```

```python
import functools
import math

import jax
import jax.numpy as jnp
from jax import lax
from jax.experimental import pallas as pl
from jax.experimental.pallas import tpu as pltpu

F32 = jnp.float32
BF16 = jnp.bfloat16

CHUNK = 64
LRU_C = 8.0
LN_EPS = 1e-5
NEG_INF = -1e30
S5_CHUNK = 64
LRU_ROWS_LONG = 256

LANES = 128
LOG2E = math.log2(math.e)

MIB = 1024 * 1024


def _params(semantics, vmem_mib):
    return pltpu.CompilerParams(dimension_semantics=semantics, vmem_limit_bytes=vmem_mib * MIB)


def _dot(a, b):
    return jnp.dot(a, b, preferred_element_type=F32)


def _dot_nt(a, b):
    return lax.dot_general(a, b, (((1,), (1,)), ((), ())), preferred_element_type=F32)


def _split_bf16(a):
    hi = a.astype(BF16)
    lo = (a - hi.astype(F32)).astype(BF16)
    return hi, lo


def _dot3(a, b):
    a_hi, a_lo = _split_bf16(a)
    b_hi, b_lo = _split_bf16(b)
    return _dot(a_hi, b_hi) + (_dot(a_lo, b_hi) + _dot(a_hi, b_lo))


def _mm_res_kernel(x_ref, w_ref, r_ref, o_ref, *, nk, alpha, scale):
    part = _dot(x_ref[...], w_ref[...])
    k = pl.program_id(2)

    @pl.when(k == 0)
    def _():
        o_ref[...] = part

    @pl.when(k > 0)
    def _():
        o_ref[...] += part

    @pl.when(k == nk - 1)
    def _():
        o_ref[...] = alpha * r_ref[...] + scale * o_ref[...]


def _matmul_res(x, w, res, *, w_lead=(), bm, bn, bk, alpha, scale, vmem_mib):
    m, kdim = x.shape
    n = w.shape[-1]
    nk = kdim // bk
    assert w.shape[-2] == kdim and m % bm == 0 and n % bn == 0 and kdim % bk == 0 and nk > 1
    tile = pl.BlockSpec((bm, bn), lambda i, j, k: (i, j))
    return pl.pallas_call(
        functools.partial(_mm_res_kernel, nk=nk, alpha=alpha, scale=scale),
        grid=(m // bm, n // bn, nk),
        in_specs=[pl.BlockSpec((bm, bk), lambda i, j, k: (i, k)),
                  pl.BlockSpec((None,) * len(w_lead) + (bk, bn), lambda i, j, k: w_lead + (k, j)), tile],
        out_specs=tile,
        out_shape=jax.ShapeDtypeStruct((m, n), F32),
        compiler_params=_params(("parallel", "parallel", "arbitrary"), vmem_mib),
    )(x, w, res)


def _mm_f32w_kernel(x_ref, w_ref, *rest, alpha, scale):
    o_ref, w_sc = rest[-2:]

    @pl.when(pl.program_id(1) == 0)
    def _():
        w_sc[...] = w_ref[...].astype(BF16)

    acc = _dot(x_ref[...], w_sc[...])
    if len(rest) == 3:
        acc = alpha * rest[0][...] + scale * acc
    o_ref[...] = acc.astype(o_ref.dtype)


def _matmul_f32w(x, w, res=None, *, w_lead=(), skip_cols=(0, 0), bm, bn, out_dtype, vmem_mib, alpha=1.0, scale=1.0):
    m, kdim = x.shape
    lo, hi = skip_cols[0] // bn, skip_cols[1] // bn
    n = w.shape[-1] - (hi - lo) * bn
    assert w.shape[-2] == kdim and m % bm == 0 and n % bn == 0 and skip_cols == (lo * bn, hi * bn)
    w_col = lambda j: j + (hi - lo) * (j >= lo).astype(jnp.int32) if hi > lo else j
    tile = pl.BlockSpec((bm, bn), lambda j, i: (i, j))
    extra = [] if res is None else [res]
    return pl.pallas_call(
        functools.partial(_mm_f32w_kernel, alpha=alpha, scale=scale),
        grid=(n // bn, m // bm),
        in_specs=[pl.BlockSpec((bm, kdim), lambda j, i: (i, 0)),
                  pl.BlockSpec((None,) * len(w_lead) + (kdim, bn), lambda j, i: w_lead + (0, w_col(j)))]
                 + [tile] * len(extra),
        out_specs=tile,
        out_shape=jax.ShapeDtypeStruct((m, n), out_dtype),
        scratch_shapes=[pltpu.VMEM((kdim, bn), BF16)],
        compiler_params=_params(("parallel", "arbitrary"), vmem_mib),
    )(x, w, *extra)


def _mm_f32w_slab_kernel(x_ref, w_ref, _, o_ref, w_sc):
    @pl.when(pl.program_id(1) == 0)
    def _():
        w_sc[...] = w_ref[...].astype(BF16)

    o_ref[...] = _dot(x_ref[...], w_sc[...])


def _matmul_f32w_slab(x, w, buf, *, w_lead, w_col0, row0, slab, bm, bn, vmem_mib):
    kdim = x.shape[1]
    _, rows, cols = buf.shape
    assert rows % bm == 0 and cols % bn == 0 and row0 % bm == 0 and w_col0 % bn == 0 and buf.dtype == F32
    return pl.pallas_call(
        _mm_f32w_slab_kernel,
        grid=(cols // bn, rows // bm),
        in_specs=[pl.BlockSpec((bm, kdim), lambda j, i: (row0 // bm + i, 0)),
                  pl.BlockSpec((None,) * len(w_lead) + (kdim, bn), lambda j, i: w_lead + (0, w_col0 // bn + j)),
                  pl.BlockSpec(memory_space=pl.ANY)],
        out_specs=pl.BlockSpec((None, bm, bn), lambda j, i: (slab, i, j)),
        out_shape=jax.ShapeDtypeStruct(buf.shape, F32),
        input_output_aliases={2: 0},
        scratch_shapes=[pltpu.VMEM((kdim, bn), BF16)],
        compiler_params=_params(("parallel", "arbitrary"), vmem_mib),
    )(x, w, buf)


def _swiglu_kernel(x_ref, wg_ref, wu_ref, o_ref, wg_sc, wu_sc, *, n_real):
    j = pl.program_id(0)

    @pl.when(jnp.logical_and(pl.program_id(1) == 0, j < n_real))
    def _():
        wg_sc[...] = wg_ref[...].astype(BF16)
        wu_sc[...] = wu_ref[...].astype(BF16)

    @pl.when(j < n_real)
    def _():
        x = x_ref[...]
        g = _dot(x, wg_sc[...])
        u = _dot(x, wu_sc[...])
        o_ref[...] = (g * jax.nn.sigmoid(g) * u).astype(o_ref.dtype)

    @pl.when(j >= n_real)
    def _():
        o_ref[...] = jnp.zeros(o_ref.shape, o_ref.dtype)


def _swiglu_in(xb, w_gu, *, w_lead, f_pad, bm, bn):
    m, kdim = xb.shape
    f = w_gu.shape[-1] // 2
    n_real = f // bn
    nj = f_pad // bn
    assert f % bn == 0 and f_pad % bn == 0 and m % bm == 0
    lead = (None,) * len(w_lead)
    col = lambda j: jnp.minimum(j, n_real - 1)
    return pl.pallas_call(
        functools.partial(_swiglu_kernel, n_real=n_real),
        grid=(nj, m // bm),
        in_specs=[pl.BlockSpec((bm, kdim), lambda j, i: (i, 0)),
                  pl.BlockSpec(lead + (kdim, bn), lambda j, i: w_lead + (0, col(j))),
                  pl.BlockSpec(lead + (kdim, bn), lambda j, i: w_lead + (0, col(j) + n_real))],
        out_specs=pl.BlockSpec((bm, bn), lambda j, i: (i, j)),
        out_shape=jax.ShapeDtypeStruct((m, f_pad), BF16),
        scratch_shapes=[pltpu.VMEM((kdim, bn), BF16), pltpu.VMEM((kdim, bn), BF16)],
        compiler_params=_params(("parallel", "arbitrary"), 48),
    )(xb, w_gu, w_gu)


def _ln_kernel(y_ref, g_ref, b_ref, o_ref, ob_ref):
    y = y_ref[...]
    mu = jnp.mean(y, axis=-1, keepdims=True)
    d = y - mu
    var = jnp.mean(d * d, axis=-1, keepdims=True)
    out = d * lax.rsqrt(var + LN_EPS) * g_ref[...] + b_ref[...]
    o_ref[...] = out
    ob_ref[...] = out.astype(BF16)


def _layer_norm(y, g, b, *, bm=256):
    m, d = y.shape
    row = pl.BlockSpec((bm, d), lambda i: (i, 0))
    vec = pl.BlockSpec((1, d), lambda i: (0, 0))
    return pl.pallas_call(
        _ln_kernel,
        grid=(m // bm,),
        in_specs=[row, vec, vec],
        out_specs=[row, row],
        out_shape=[jax.ShapeDtypeStruct((m, d), F32), jax.ShapeDtypeStruct((m, d), BF16)],
        compiler_params=_params(("parallel",), 40),
    )(y, g.reshape(1, d), b.reshape(1, d))


def _s5_kernel(u_ref, h0_ref, min_ref, minsw_ref, kseq_ref, mout_ref, dec_ref, y_ref, hfin_ref,
               s_sc, ssw_sc, hprev_sc, toep_sc, *, n_seq, n_chunk, n_single):
    ch, seq_w = kseq_ref.shape
    width = toep_sc.shape[1]
    per_tile = LANES // ch
    kseq = kseq_ref[...]
    for b in range(per_tile):
        rot = kseq if b == 0 else pltpu.roll(kseq, seq_w - ch * b, axis=1)
        for a in range(width // LANES):
            sigma = width // ch - 1 - (a * per_tile + b)
            toep_sc[sigma * ch:(sigma + 1) * ch, :] = rot[:, a * LANES:a * LANES + width]

    u = u_ref[...]
    u_hi, u_lo = _split_bf16(u)

    def dot3_u(w):
        w_hi, w_lo = _split_bf16(w)
        return _dot(u_hi, w_hi) + (_dot(u_lo, w_hi) + _dot(u_hi, w_lo))

    s_sc[...] = dot3_u(min_ref[...])
    ssw_sc[...] = dot3_u(minsw_ref[...])
    a1 = dec_ref[0:1, :]
    a2 = dec_ref[1:2, :]
    a2sw = dec_ref[2:3, :]

    h = jnp.zeros((n_seq, s_sc.shape[1]), F32)
    hsw = h
    for k in range(n_chunk):
        rows = slice(k * n_seq, (k + 1) * n_seq)
        hprev_sc[rows, :] = h
        h, hsw = (a1 * h + a2 * hsw + s_sc[rows, :], a1 * hsw + a2sw * h + ssw_sc[rows, :])
    n_chain = n_chunk * n_seq
    hfin_ref[0:n_seq, :] = h
    h0 = h0_ref[...]
    h0sw = pltpu.roll(h0, h0.shape[1] // 2, axis=1)
    hprev_sc[n_chain:n_chain + n_single, :] = h0
    hfin_ref[n_seq:n_seq + n_single, :] = a1 * h0 + a2 * h0sw + s_sc[n_chain:n_chain + n_single, :]

    y_ref[...] = dot3_u(toep_sc[...]) + _dot3(hprev_sc[...], mout_ref[...])


def _s5_scan(u_g, h0_g, mats, *, n_seq, n_chunk, n_single):
    m_in, m_in_sw, kseq, m_out, dec = mats
    g, rows, width = u_g.shape
    p2 = m_in.shape[2]
    n_out = n_seq + n_single
    grp = lambda *shape: pl.BlockSpec((None,) + shape, lambda i: (i,) + (0,) * len(shape))
    return pl.pallas_call(
        functools.partial(_s5_kernel, n_seq=n_seq, n_chunk=n_chunk, n_single=n_single),
        grid=(g,),
        in_specs=[grp(rows, width), grp(n_single, p2), grp(width, p2), grp(width, p2),
                  grp(*kseq.shape[1:]), grp(p2, width), grp(3, p2)],
        out_specs=[grp(rows, width), grp(n_out, p2)],
        out_shape=[jax.ShapeDtypeStruct((g, rows, width), F32),
                   jax.ShapeDtypeStruct((g, n_out, p2), F32)],
        scratch_shapes=[pltpu.VMEM((rows, p2), F32), pltpu.VMEM((rows, p2), F32),
                        pltpu.VMEM((rows, p2), F32), pltpu.VMEM((width, width), F32)],
        compiler_params=_params(("parallel",), 40),
    )(u_g, h0_g, m_in, m_in_sw, kseq, m_out, dec)


def _s5_matrices(lam_re, lam_im, log_step, b_re, b_im, c_re, c_im):
    hp = lax.Precision.HIGHEST
    L = S5_CHUNK
    g, p = lam_re.shape
    ch = b_re.shape[2]
    lam = lax.complex(lam_re.astype(F32), lam_im.astype(F32))
    step = jnp.exp(log_step.astype(F32))[:, None]
    lam_step = lam * step
    lam_bar = jnp.exp(lam_step)
    b_bar = ((lam_bar - 1.0) / lam)[..., None] * lax.complex(b_re.astype(F32), b_im.astype(F32))
    c_mat = lax.complex(c_re.astype(F32), c_im.astype(F32))
    d = jnp.arange(L + 1, dtype=F32)
    pw = jnp.exp(lam_step[None] * d[:, None, None])
    w_in = pw[:L][::-1].transpose(1, 0, 2)[:, :, None, :] * b_bar.transpose(0, 2, 1)[:, None, :, :]
    w_in = w_in.reshape(g, L * ch, p)
    m_in = jnp.concatenate([jnp.real(w_in), jnp.imag(w_in)], axis=-1)
    m_in_sw = jnp.concatenate([jnp.imag(w_in), jnp.real(w_in)], axis=-1)
    w_out = pw[1:].transpose(1, 2, 0)[:, :, :, None] * c_mat.transpose(0, 2, 1)[:, :, None, :]
    w_out = w_out.reshape(g, p, L * ch)
    m_out = jnp.concatenate([jnp.real(w_out), -jnp.imag(w_out)], axis=1)
    cp = c_mat[None] * pw[:L][:, :, None, :]
    kd = (jnp.einsum('dgcp,gpe->dgce', jnp.real(cp), jnp.real(b_bar), precision=hp)
          - jnp.einsum('dgcp,gpe->dgce', jnp.imag(cp), jnp.imag(b_bar), precision=hp))
    kseq = jnp.pad(kd.transpose(1, 3, 0, 2), ((0, 0), (0, 0), (L - 1, 1), (0, 0))).reshape(g, ch, 2 * L * ch)
    pl_ = pw[L]
    dec = jnp.stack([jnp.concatenate([jnp.real(pl_), jnp.real(pl_)], -1),
                     jnp.concatenate([-jnp.imag(pl_), jnp.imag(pl_)], -1),
                     jnp.concatenate([jnp.imag(pl_), -jnp.imag(pl_)], -1)], axis=1)
    return m_in, m_in_sw, kseq, m_out, dec


def _glu_kernel(y1_ref, y2_ref, u_ref, d_ref, w_ref, b_ref, o_ref, *, n_first):
    def run(y_ref):
        y = jax.nn.gelu(y_ref[...] + d_ref[...] * u_ref[...])
        z = _dot(y.astype(BF16), w_ref[...]) + b_ref[...]
        o_ref[...] = (y * jax.nn.sigmoid(z)).astype(BF16)

    pl.when(pl.program_id(0) < n_first)(lambda: run(y1_ref))
    pl.when(pl.program_id(0) >= n_first)(lambda: run(y2_ref))


def _s5_glu(y1, y2, proj, d_skip, w_glu, b_glu, *, bm=512):
    (m1, w), m2 = y1.shape, y2.shape[0]
    assert m1 % bm == 0 and m2 % bm == 0
    n_first = m1 // bm
    row = pl.BlockSpec((bm, w), lambda i: (i, 0))
    vec = pl.BlockSpec((1, w), lambda i: (0, 0))
    return pl.pallas_call(
        functools.partial(_glu_kernel, n_first=n_first),
        grid=((m1 + m2) // bm,),
        in_specs=[pl.BlockSpec((bm, w), lambda i: (jnp.minimum(i, n_first - 1), 0)),
                  pl.BlockSpec((bm, w), lambda i: (jnp.maximum(i - n_first, 0), 0)),
                  row, vec, pl.BlockSpec((w, w), lambda i: (0, 0)), vec],
        out_specs=row,
        out_shape=jax.ShapeDtypeStruct((m1 + m2, w), BF16),
        compiler_params=_params(("parallel",), 32),
    )(y1, y2, proj, d_skip.reshape(1, w), w_glu, b_glu.reshape(1, w))


def _lru_kernel(x_ref, gate_ref, h0_ref, c0_ref, cw_ref, cb_ref, wa_ref, ba_ref, wx_ref, bx_ref, sp_ref, _,
                y_ref, hout_ref, cout_ref, xpad_sc, a_sc, b_sc, hs_sc, h_sc,
                *, rows, blocks_per_seq, conv_width):
    tail = conv_width - 1
    base = 8

    @pl.when(pl.program_id(0) % blocks_per_seq == 0)
    def _():
        h_sc[...] = h0_ref[...]
        xpad_sc[base - tail:base, :] = c0_ref[...]

    x = x_ref[...]
    xpad_sc[base:base + rows, :] = x
    xc = cb_ref[...]
    for j in range(conv_width):
        off = base - tail + j
        xc = xc + cw_ref[j:j + 1, :] * xpad_sc[off:off + rows, :]
    new_tail = x_ref[rows - tail:rows, :]
    xpad_sc[base - tail:base, :] = new_tail
    cout_ref[...] = new_tail

    xcb = xc.astype(BF16)
    r = jax.nn.sigmoid(_dot(xcb, wa_ref[...]) + ba_ref[...])
    ig = jax.nn.sigmoid(_dot(xcb, wx_ref[...]) + bx_ref[...])
    log_a = -LRU_C * r * sp_ref[...]
    a = jnp.exp(log_a)
    mult = jnp.sqrt(-jnp.tanh(log_a) * (a * a + 1.0))
    a_sc[...] = a
    b_sc[...] = mult * (ig * xc)

    def step(t, h):
        h = a_sc[pl.ds(t, 1), :] * h + b_sc[pl.ds(t, 1), :]
        hs_sc[pl.ds(t, 1), :] = h
        return h

    h = lax.fori_loop(0, rows, step, h_sc[...], unroll=8)
    h_sc[...] = h
    hout_ref[...] = h
    y_ref[...] = (hs_sc[...] * jax.nn.gelu(gate_ref[...])).astype(BF16)


def _rglru(proj, h0, c0, conv_w, conv_b, wa_bd, b_a, wx_bd, b_x, sp, y_buf, *, rows, row0, blocks_per_seq,
           x_col, gate_col):
    n_seq, width = h0.shape
    cw = conv_w.shape[0]
    assert row0 % rows == 0 and y_buf.shape[1] == width and y_buf.dtype == BF16
    blk0 = row0 // rows
    vec = pl.BlockSpec((1, width), lambda i: (0, 0))
    mat = pl.BlockSpec((width, width), lambda i: (0, 0))
    hspec = pl.BlockSpec((None, 1, width), lambda i: (i // blocks_per_seq, 0, 0))
    cspec = pl.BlockSpec((None, cw - 1, width), lambda i: (i // blocks_per_seq, 0, 0))
    return pl.pallas_call(
        functools.partial(_lru_kernel, rows=rows, blocks_per_seq=blocks_per_seq, conv_width=cw),
        grid=(n_seq * blocks_per_seq,),
        in_specs=[pl.BlockSpec((rows, width), lambda i: (blk0 + i, x_col)),
                  pl.BlockSpec((rows, width), lambda i: (blk0 + i, gate_col)),
                  hspec, cspec, pl.BlockSpec((cw, width), lambda i: (0, 0)), vec, mat, vec, mat, vec, vec,
                  pl.BlockSpec(memory_space=pl.ANY)],
        out_specs=[pl.BlockSpec((rows, width), lambda i: (blk0 + i, 0)), hspec, cspec],
        out_shape=[jax.ShapeDtypeStruct(y_buf.shape, BF16),
                   jax.ShapeDtypeStruct((n_seq, 1, width), F32),
                   jax.ShapeDtypeStruct((n_seq, cw - 1, width), F32)],
        input_output_aliases={11: 0},
        scratch_shapes=[pltpu.VMEM((rows + 8, width), F32), pltpu.VMEM((rows, width), F32),
                        pltpu.VMEM((rows, width), F32), pltpu.VMEM((rows, width), F32),
                        pltpu.VMEM((1, width), F32)],
        compiler_params=_params(("arbitrary",), 32),
    )(proj, proj, h0.reshape(n_seq, 1, width), c0, conv_w, conv_b.reshape(1, width), wa_bd, b_a.reshape(1, width),
      wx_bd, b_x.reshape(1, width), sp.reshape(1, width), y_buf)


def _block_diag(w):
    n, c, d = w.shape
    eye = jnp.eye(n, dtype=w.dtype)
    return (eye[:, None, :, None] * w[:, :, None, :]).reshape(n * c, n * d)


def _head_norm(o, g, out_scale):
    o = o * lax.rsqrt(jnp.mean(o * o, axis=-1, keepdims=True) + LN_EPS) * g
    return o * out_scale


def _attn_prompt_kernel(scal_ref, sbt_ref, q_ref, k_ref, v_ref, g_ref, _, o_ref, m_sc, l_sc, acc_sc, doff_sc, ddiag_sc,
                        *, blk, hd, n_heads, out_scale):
    h = pl.program_id(1)
    i = pl.program_id(2)
    nq = pl.num_programs(2)
    slope2 = scal_ref[h]
    lam = scal_ref[n_heads]
    c1 = (hd ** -0.5) * LOG2E
    shift = int(math.log2(CHUNK))
    e = 2 * hd
    sub = blk // 2

    @pl.when(i == 0)
    def _():
        r = lax.broadcasted_iota(jnp.int32, (blk, blk), 0)
        c = lax.broadcasted_iota(jnp.int32, (blk, blk), 1)
        rel = r - c
        doff_sc[...] = slope2 * rel.astype(F32)
        visible = lax.shift_right_arithmetic(c, shift) <= lax.shift_right_arithmetic(r, shift)
        ddiag_sc[...] = jnp.where(visible, slope2 * jnp.abs(rel).astype(F32), -NEG_INF)

    q = q_ref[...].astype(BF16)
    m_sc[...] = jnp.full(m_sc.shape, NEG_INF, F32)
    l_sc[...] = jnp.zeros(l_sc.shape, F32)
    acc_sc[...] = jnp.zeros(acc_sc.shape, F32)

    def process(j, d_ref, sb):
        start = pl.multiple_of(j * blk, blk)
        kj = k_ref[pl.ds(start, blk), :].astype(BF16)
        vj = v_ref[pl.ds(start, blk), :].astype(BF16)
        for c, r in [(c, r) for r in range(blk // sub) for c in range(2)]:
            cols = slice(c * hd, (c + 1) * hd)
            rows = slice(r * sub, (r + 1) * sub)
            t = _dot_nt(q[rows, cols], kj[:, cols]) * c1 - d_ref[rows, :]
            m_prev = m_sc[c, rows, :]
            m_next = jnp.maximum(m_prev, jnp.max(t, axis=1, keepdims=True) - sb)
            p = jnp.exp2(t - jnp.tile(m_next + sb, (1, blk // LANES)))
            corr = jnp.exp2(m_prev - m_next)
            psum = p[:, 0:LANES]
            for w in range(1, blk // LANES):
                psum = psum + p[:, w * LANES:(w + 1) * LANES]
            l_sc[c, rows, :] = corr * l_sc[c, rows, :] + psum
            acc_sc[c, rows, :] = acc_sc[c, rows, :] * jnp.tile(corr, (1, e // LANES)) + _dot(p.astype(BF16), vj)
            m_sc[c, rows, :] = m_next

    def body(j, carry):
        process(j, doff_sc, sbt_ref[h * nq + (i - j)])
        return carry

    lax.fori_loop(0, i, body, 0)
    process(i, ddiag_sc, 0.0)
    outs = [acc_sc[c] / jnp.sum(l_sc[c], axis=1, keepdims=True) for c in range(2)]
    o = outs[0] - lam * outs[1]
    o_ref[...] = _head_norm(o, g_ref[...], out_scale).astype(BF16)


def _attn_prompt(proj, k_all, v_all, slopes, lam, subln, y_buf, *, layer, n_batch, seq, n_heads, hd, q_col,
                 out_scale, blk=512):
    assert blk % CHUNK == 0 and blk % LANES == 0 and seq % blk == 0
    assert y_buf.shape[1] == n_heads * 2 * hd and y_buf.dtype == BF16
    nq = seq // blk
    e = 2 * hd
    scal = jnp.concatenate([slopes * LOG2E, lam.reshape(1)])
    sb_tab = (slopes[:, None] * (LOG2E * blk * jnp.arange(nq, dtype=F32))[None, :]).reshape(-1)
    smem = pl.BlockSpec(memory_space=pltpu.SMEM)
    return pl.pallas_call(
        functools.partial(_attn_prompt_kernel, blk=blk, hd=hd, n_heads=n_heads, out_scale=out_scale),
        grid=(n_batch, n_heads, nq),
        in_specs=[smem, smem,
                  pl.BlockSpec((blk, e), lambda b, h, i: (b * nq + i, q_col + h)),
                  pl.BlockSpec((None, seq, e), lambda b, h, i: (layer, b, h)),
                  pl.BlockSpec((None, seq, e), lambda b, h, i: (layer, b, h)),
                  pl.BlockSpec((1, e), lambda b, h, i: (0, 0)), pl.BlockSpec(memory_space=pl.ANY)],
        out_specs=pl.BlockSpec((blk, e), lambda b, h, i: (b * nq + i, h)),
        out_shape=jax.ShapeDtypeStruct(y_buf.shape, BF16),
        input_output_aliases={6: 0},
        scratch_shapes=[pltpu.VMEM((2, blk, LANES), F32), pltpu.VMEM((2, blk, LANES), F32),
                        pltpu.VMEM((2, blk, e), F32), pltpu.VMEM((blk, blk), F32), pltpu.VMEM((blk, blk), F32)],
        compiler_params=_params(("parallel", "parallel", "arbitrary"), 48),
    )(scal, sb_tab, proj, k_all, v_all, subln.reshape(1, e), y_buf)


def _attn_sample_kernel(scal_ref, q_ref, kn_ref, vn_ref, g_ref, kc_hbm, vc_hbm, _, o_ref, kbuf, vbuf, sem,
                        *, layer, hd, n_heads, past, out_scale):
    b = pl.program_id(0)
    h = pl.program_id(1)
    step = b * n_heads + h
    n_steps = pl.num_programs(0) * n_heads
    slot = lax.rem(step, 2)

    def cache_copies(at_step, at_slot):
        bb = at_step // n_heads
        hh = lax.rem(at_step, n_heads)
        return (pltpu.make_async_copy(kc_hbm.at[layer, bb, :, hh, :], kbuf.at[at_slot], sem.at[0, at_slot]),
                pltpu.make_async_copy(vc_hbm.at[layer, bb, :, hh, :], vbuf.at[at_slot], sem.at[1, at_slot]))

    @pl.when(step == 0)
    def _():
        for cp in cache_copies(step, slot):
            cp.start()

    @pl.when(step + 1 < n_steps)
    def _():
        for cp in cache_copies(step + 1, 1 - slot):
            cp.start()

    slope = scal_ref[h]
    lam = scal_ref[n_heads]
    scale = hd ** -0.5
    shift = int(math.log2(CHUNK))
    t = q_ref.shape[0]
    q = q_ref[...].astype(BF16)
    kn = kn_ref[...].astype(BF16)
    vn = vn_ref[...].astype(BF16)
    qpos = past + lax.broadcasted_iota(jnp.int32, (t, 1), 0)
    bias_c = slope * (qpos - lax.broadcasted_iota(jnp.int32, (1, past), 1)).astype(F32)
    kposn = past + lax.broadcasted_iota(jnp.int32, (1, t), 1)
    bias_n = slope * jnp.abs(qpos - kposn).astype(F32)
    vis_n = lax.shift_right_arithmetic(kposn, shift) <= lax.shift_right_arithmetic(qpos, shift)

    for cp in cache_copies(step, slot):
        cp.wait()
    kc = kbuf[slot].astype(BF16)
    vc = vbuf[slot].astype(BF16)
    outs = []
    for c in range(2):
        cols = slice(c * hd, (c + 1) * hd)
        s_c = _dot_nt(q[:, cols], kc[:, cols]) * scale - bias_c
        s_n = jnp.where(vis_n, _dot_nt(q[:, cols], kn[:, cols]) * scale - bias_n, NEG_INF)
        m = jnp.maximum(jnp.max(s_c, axis=-1, keepdims=True), jnp.max(s_n, axis=-1, keepdims=True))
        p_c = jnp.exp(s_c - m)
        p_n = jnp.exp(s_n - m)
        l = jnp.sum(p_c, axis=-1, keepdims=True) + jnp.sum(p_n, axis=-1, keepdims=True)
        outs.append((_dot(p_c.astype(BF16), vc) + _dot(p_n.astype(BF16), vn)) / l)
    o = outs[0] - lam * outs[1]
    o_ref[...] = _head_norm(o, g_ref[...], out_scale).astype(BF16)


def _attn_sample(proj, k_new, v_new, cache_k, cache_v, scal, subln, y_buf, *, layer, n_batch, seq, n_heads, hd,
                 q_col, row0, out_scale):
    e = 2 * hd
    past = cache_k.shape[2]
    assert y_buf.shape[1] == n_heads * e and y_buf.dtype == BF16 and row0 % seq == 0
    rb0 = row0 // seq
    new = pl.BlockSpec((None, seq, e), lambda b, h: (layer, b, h))
    hbm = pl.BlockSpec(memory_space=pl.ANY)
    return pl.pallas_call(
        functools.partial(_attn_sample_kernel, layer=layer, hd=hd, n_heads=n_heads, past=past, out_scale=out_scale),
        grid=(n_batch, n_heads),
        in_specs=[pl.BlockSpec(memory_space=pltpu.SMEM), pl.BlockSpec((seq, e), lambda b, h: (rb0 + b, q_col + h)),
                  new, new, pl.BlockSpec((1, e), lambda b, h: (0, 0)), hbm, hbm, hbm],
        out_specs=pl.BlockSpec((seq, e), lambda b, h: (rb0 + b, h)),
        out_shape=jax.ShapeDtypeStruct(y_buf.shape, BF16),
        input_output_aliases={7: 0},
        scratch_shapes=[pltpu.VMEM((2, past, e), F32), pltpu.VMEM((2, past, e), F32),
                        pltpu.SemaphoreType.DMA((2, 2))],
        compiler_params=_params(("arbitrary", "arbitrary"), 40),
    )(scal, proj, k_new, v_new, subln.reshape(1, e), cache_k, cache_v, y_buf)


def _merge_kernel(ya_ref, yb_ref, yc_ref, wa_ref, wb_ref, wc_ref, ga_ref, gb_ref, gc_ref, bg_ref, o_ref):
    def branch(y_ref, w_ref, g_ref, r):
        return jax.nn.sigmoid(g_ref[...] + bg_ref[r:r + 1, :]) * _dot(y_ref[...], w_ref[...])

    merged = branch(ya_ref, wa_ref, ga_ref, 0) + branch(yb_ref, wb_ref, gb_ref, 1) + branch(yc_ref, wc_ref, gc_ref, 2)
    o_ref[...] = merged.astype(BF16)


def _merge(ya, yb, yc, wa, wb, wc, proj, b_gate, *, gate_col, bm=512, bn=1024):
    m = ya.shape[0]
    d = wa.shape[1]
    nj = d // bn
    y_spec = lambda y: pl.BlockSpec((bm, y.shape[1]), lambda j, i: (i, 0))
    w_spec = lambda w: pl.BlockSpec((w.shape[0], bn), lambda j, i: (0, j))
    g_spec = lambda r: pl.BlockSpec((bm, bn), lambda j, i: (i, gate_col + r * nj + j))
    return pl.pallas_call(
        _merge_kernel,
        grid=(nj, m // bm),
        in_specs=[y_spec(ya), y_spec(yb), y_spec(yc), w_spec(wa), w_spec(wb), w_spec(wc),
                  g_spec(0), g_spec(1), g_spec(2), pl.BlockSpec((3, bn), lambda j, i: (0, j))],
        out_specs=pl.BlockSpec((bm, bn), lambda j, i: (i, j)),
        out_shape=jax.ShapeDtypeStruct((m, d), BF16),
        compiler_params=_params(("parallel", "parallel"), 48),
    )(ya, yb, yc, wa, wb, wc, proj, proj, proj, b_gate)


def _ple_kernel(xb_ref, wg_ref, p_ref, wp_ref, x_ref, o_ref, ob_ref):
    gate = jax.nn.sigmoid(_dot(xb_ref[...], wg_ref[...]))
    out = x_ref[...] + gate * _dot(p_ref[...], wp_ref[...])
    o_ref[...] = out
    ob_ref[...] = out.astype(BF16)


def _ple(x, xb, pb, w_gate, w_proj, *, bm=512, bn=1024):
    m, d = x.shape
    pdim = pb.shape[1]
    tile = pl.BlockSpec((bm, bn), lambda j, i: (i, j))
    return pl.pallas_call(
        _ple_kernel,
        grid=(d // bn, m // bm),
        in_specs=[pl.BlockSpec((bm, d), lambda j, i: (i, 0)), pl.BlockSpec((d, bn), lambda j, i: (0, j)),
                  pl.BlockSpec((bm, pdim), lambda j, i: (i, 0)), pl.BlockSpec((pdim, bn), lambda j, i: (0, j)),
                  tile],
        out_specs=[tile, tile],
        out_shape=[jax.ShapeDtypeStruct((m, d), F32), jax.ShapeDtypeStruct((m, d), BF16)],
        compiler_params=_params(("parallel", "parallel"), 48),
    )(xb, w_gate, pb, w_proj, x)


def kernel(x_prompt, x_sample, cache_k, cache_v, state_s5_re, state_s5_im, state_lru, state_conv, p_prompt, p_sample, ln_g, ln_b, ffn_w_in, ffn_w_out, w_in, b_gate, s5_lam_re, s5_lam_im, s5_log_step, s5_b_re, s5_b_im, s5_c_re, s5_c_im, s5_d, s5_w_glu, s5_b_glu, diff_lambda, diff_subln, lru_conv_w, lru_conv_b, lru_w_a, lru_b_a, lru_w_x, lru_b_x, lru_lambda, w_br_a, w_br_b, w_br_c, w_o, ple_w_proj, ple_w_gate):
    bp, tp, d_model = x_prompt.shape
    bs, ts, _ = x_sample.shape
    depth = ln_g.shape[0]
    mp, ms = bp * tp, bs * ts
    m = mp + ms
    n_heads, e = cache_k.shape[3], cache_k.shape[4]
    hd = e // 2
    past = cache_k.shape[2]
    groups, p_state, grp_ch = s5_b_re.shape[1:]
    s5_w = groups * grp_ch
    att_w = n_heads * e
    lru_w = lru_lambda.shape[1]
    d_ff = ffn_w_out.shape[2]
    f_pad = -(-d_ff // 1024) * 1024
    alpha = (2 * depth) ** 0.25
    L = S5_CHUNK
    assert ts == L and tp % L == 0 and tp % LRU_ROWS_LONG == 0 and ts % 8 == 0 and ts == CHUNK and past % CHUNK == 0
    kp = tp // L
    w_k, w_v = s5_w + att_w, s5_w + 2 * att_w
    c_q = s5_w
    c_xr = s5_w + att_w
    c_gr = c_xr + lru_w
    c_gl = c_gr + lru_w
    kv_bufs = [jnp.zeros((depth, rows, att_w), F32) for rows in (mp, mp, ms, ms)]

    x = jnp.concatenate([x_prompt.reshape(mp, d_model), x_sample.reshape(ms, d_model)], axis=0)
    xb = x.astype(BF16)
    slopes = jnp.exp2(-8.0 * (jnp.arange(n_heads, dtype=F32) + 1.0) / n_heads)

    w_dn_all = jnp.pad(ffn_w_out.astype(BF16), ((0, 0), (0, 0), (0, f_pad - d_ff), (0, 0)))

    def ffn(i, s, x, xb):
        hdn = _swiglu_in(xb, ffn_w_in, w_lead=(i, s), f_pad=f_pad, bm=1024, bn=256)
        y = _matmul_res(hdn, w_dn_all, x, w_lead=(i, s), bm=1024, bn=1024, bk=f_pad // 4, alpha=alpha, scale=0.5,
                        vmem_mib=48)
        return _layer_norm(y, ln_g[i, 2 * s], ln_b[i, 2 * s])

    s5r, s5i, lruh, convs = [], [], [], []
    for i in range(depth):
        x, xb = ffn(i, 0, x, xb)

        proj = _matmul_f32w(xb, w_in, w_lead=(i,), skip_cols=(w_k, w_v + att_w), bm=1024, bn=512, out_dtype=F32,
                            vmem_mib=48)
        kv_bufs = [_matmul_f32w_slab(xb, w_in, buf, w_lead=(i,), w_col0=col, row0=row0, slab=i, bm=1024, bn=512,
                                     vmem_mib=48)
                   for buf, col, row0 in zip(kv_bufs, (w_k, w_v, w_k, w_v), (0, 0, mp, mp))]
        k_p, v_p, k_s, v_s = kv_bufs

        u_p = proj[:mp, :s5_w].reshape(bp, kp, L, groups, grp_ch).transpose(3, 1, 0, 2, 4).reshape(groups, kp * bp, L * grp_ch)
        u_s = proj[mp:, :s5_w].reshape(bs, L, groups, grp_ch).transpose(2, 0, 1, 3).reshape(groups, bs, L * grp_ch)
        u_g = jnp.concatenate([u_p, u_s], axis=1)
        h0_g = jnp.concatenate([state_s5_re[i], state_s5_im[i]], axis=-1).transpose(1, 0, 2)
        mats = _s5_matrices(s5_lam_re[i], s5_lam_im[i], s5_log_step[i], s5_b_re[i], s5_b_im[i], s5_c_re[i], s5_c_im[i])
        y_g, hfin = _s5_scan(u_g, h0_g, mats, n_seq=bp, n_chunk=kp, n_single=bs)
        y_p = y_g[:, :kp * bp].reshape(groups, kp, bp, L, grp_ch).transpose(2, 1, 3, 0, 4).reshape(mp, s5_w)
        y_s = y_g[:, kp * bp:].reshape(groups, bs, L, grp_ch).transpose(1, 2, 0, 3).reshape(ms, s5_w)
        y_a = _s5_glu(y_p, y_s, proj, s5_d[i], s5_w_glu[i].astype(BF16), s5_b_glu[i])
        hfin = hfin.transpose(1, 0, 2)
        s5r.append((hfin[:bp, :, :p_state], hfin[bp:, :, :p_state]))
        s5i.append((hfin[:bp, :, p_state:], hfin[bp:, :, p_state:]))

        lam_init = 0.8 - 0.6 * math.exp(-0.3 * i)
        dl = diff_lambda[i].astype(F32)
        lam = jnp.exp(jnp.sum(dl[0] * dl[1])) - jnp.exp(jnp.sum(dl[2] * dl[3])) + lam_init
        scal = jnp.concatenate([slopes, lam.reshape(1)])
        y_b = _attn_prompt(proj, k_p, v_p, slopes, lam, diff_subln[i], jnp.zeros((m, att_w), BF16), layer=i,
                           n_batch=bp, seq=tp, n_heads=n_heads, hd=hd, out_scale=1.0 - lam_init, q_col=c_q // e)
        y_b = _attn_sample(proj, k_s, v_s, cache_k, cache_v, scal, diff_subln[i], y_b, layer=i, n_batch=bs, seq=ts,
                           n_heads=n_heads, hd=hd, row0=mp, out_scale=1.0 - lam_init, q_col=c_q // e)

        lru_params = (lru_conv_w[i], lru_conv_b[i], _block_diag(lru_w_a[i]).astype(BF16), lru_b_a[i].reshape(-1),
                      _block_diag(lru_w_x[i]).astype(BF16), lru_b_x[i].reshape(-1),
                      jax.nn.softplus(-lru_lambda[i].astype(F32)))
        lru_cols = dict(x_col=c_xr // lru_w, gate_col=c_gr // lru_w)
        y_c, h_p, c_p = _rglru(proj, jnp.zeros((bp, lru_w), F32), jnp.zeros((bp,) + state_conv.shape[2:], F32),
                               *lru_params, jnp.zeros((m, lru_w), BF16), rows=LRU_ROWS_LONG, row0=0,
                               blocks_per_seq=tp // LRU_ROWS_LONG, **lru_cols)
        y_c, h_s, c_s = _rglru(proj, state_lru[i], state_conv[i], *lru_params, y_c, rows=ts, row0=mp,
                               blocks_per_seq=1, **lru_cols)
        lruh.append((h_p[:, 0], h_s[:, 0]))
        convs.append((c_p, c_s))

        merged = _merge(y_a, y_b, y_c, w_br_a[i].astype(BF16), w_br_b[i].astype(BF16), w_br_c[i].astype(BF16),
                        proj, b_gate[i], gate_col=c_gl // 1024)
        y = _matmul_f32w(merged, w_o, x, w_lead=(i,), bm=1024, bn=512, out_dtype=F32, vmem_mib=48, alpha=alpha)
        x, xb = _layer_norm(y, ln_g[i, 1], ln_b[i, 1])

        x, xb = ffn(i, 1, x, xb)

        pb = jnp.concatenate([p_prompt[i].reshape(mp, -1), p_sample[i].reshape(ms, -1)], axis=0).astype(BF16)
        x, xb = _ple(x, xb, pb, ple_w_gate[i].astype(BF16), ple_w_proj[i].astype(BF16))


    stack = lambda pairs, which: jnp.stack([pr[which] for pr in pairs])
    return (x[:mp].reshape(bp, tp, d_model), x[mp:].reshape(bs, ts, d_model),
            k_p.reshape(depth, bp, tp, n_heads, e), v_p.reshape(depth, bp, tp, n_heads, e),
            stack(s5r, 0), stack(s5i, 0), stack(lruh, 0), stack(convs, 0),
            k_s.reshape(depth, bs, ts, n_heads, e), v_s.reshape(depth, bs, ts, n_heads, e),
            stack(s5r, 1), stack(s5i, 1), stack(lruh, 1), stack(convs, 1))
```

```python
import functools
import math

import jax
import jax.numpy as jnp
from jax import lax
from jax.experimental import pallas as pl
from jax.experimental.pallas import tpu as pltpu

F32 = jnp.float32
BF16 = jnp.bfloat16

CHUNK = 64
LRU_C = 8.0
LN_EPS = 1e-5
NEG_INF = -1e30
S5_CHUNK = 64
LRU_ROWS_LONG = 256

LANES = 128
LOG2E = math.log2(math.e)

MIB = 1024 * 1024


def _params(semantics, vmem_mib):
    return pltpu.CompilerParams(dimension_semantics=semantics, vmem_limit_bytes=vmem_mib * MIB)


def _dot(a, b):
    return jnp.dot(a, b, preferred_element_type=F32)


def _dot_nt(a, b):
    return lax.dot_general(a, b, (((1,), (1,)), ((), ())), preferred_element_type=F32)


def _split_bf16(a):
    hi = a.astype(BF16)
    lo = (a - hi.astype(F32)).astype(BF16)
    return hi, lo


def _dot3(a, b):
    a_hi, a_lo = _split_bf16(a)
    b_hi, b_lo = _split_bf16(b)
    return _dot(a_hi, b_hi) + (_dot(a_lo, b_hi) + _dot(a_hi, b_lo))


def _mm_res_kernel(x_ref, w_ref, r_ref, o_ref, *, nk, alpha, scale):
    part = _dot(x_ref[...], w_ref[...])
    k = pl.program_id(2)

    @pl.when(k == 0)
    def _():
        o_ref[...] = part

    @pl.when(k > 0)
    def _():
        o_ref[...] += part

    @pl.when(k == nk - 1)
    def _():
        o_ref[...] = alpha * r_ref[...] + scale * o_ref[...]


def _matmul_res(x, w, res, *, w_lead=(), bm, bn, bk, alpha, scale, vmem_mib):
    m, kdim = x.shape
    n = w.shape[-1]
    nk = kdim // bk
    assert w.shape[-2] == kdim and m % bm == 0 and n % bn == 0 and kdim % bk == 0 and nk > 1
    tile = pl.BlockSpec((bm, bn), lambda i, j, k: (i, j))
    return pl.pallas_call(
        functools.partial(_mm_res_kernel, nk=nk, alpha=alpha, scale=scale),
        grid=(m // bm, n // bn, nk),
        in_specs=[pl.BlockSpec((bm, bk), lambda i, j, k: (i, k)),
                  pl.BlockSpec((None,) * len(w_lead) + (bk, bn), lambda i, j, k: w_lead + (k, j)), tile],
        out_specs=tile,
        out_shape=jax.ShapeDtypeStruct((m, n), F32),
        compiler_params=_params(("parallel", "parallel", "arbitrary"), vmem_mib),
    )(x, w, res)


def _mm_f32w_kernel(x_ref, w_ref, *rest, alpha, scale):
    o_ref, w_sc = rest[-2:]

    @pl.when(pl.program_id(1) == 0)
    def _():
        w_sc[...] = w_ref[...].astype(BF16)

    acc = _dot(x_ref[...], w_sc[...])
    if len(rest) == 3:
        acc = alpha * rest[0][...] + scale * acc
    o_ref[...] = acc.astype(o_ref.dtype)


def _matmul_f32w(x, w, res=None, *, w_lead=(), skip_cols=(0, 0), bm, bn, out_dtype, vmem_mib, alpha=1.0, scale=1.0):
    m, kdim = x.shape
    lo, hi = skip_cols[0] // bn, skip_cols[1] // bn
    n = w.shape[-1] - (hi - lo) * bn
    assert w.shape[-2] == kdim and m % bm == 0 and n % bn == 0 and skip_cols == (lo * bn, hi * bn)
    w_col = lambda j: j + (hi - lo) * (j >= lo).astype(jnp.int32) if hi > lo else j
    tile = pl.BlockSpec((bm, bn), lambda j, i: (i, j))
    extra = [] if res is None else [res]
    return pl.pallas_call(
        functools.partial(_mm_f32w_kernel, alpha=alpha, scale=scale),
        grid=(n // bn, m // bm),
        in_specs=[pl.BlockSpec((bm, kdim), lambda j, i: (i, 0)),
                  pl.BlockSpec((None,) * len(w_lead) + (kdim, bn), lambda j, i: w_lead + (0, w_col(j)))]
                 + [tile] * len(extra),
        out_specs=tile,
        out_shape=jax.ShapeDtypeStruct((m, n), out_dtype),
        scratch_shapes=[pltpu.VMEM((kdim, bn), BF16)],
        compiler_params=_params(("parallel", "arbitrary"), vmem_mib),
    )(x, w, *extra)


def _mm_f32w_slab_kernel(x_ref, w_ref, _, o_ref, w_sc):
    @pl.when(pl.program_id(1) == 0)
    def _():
        w_sc[...] = w_ref[...].astype(BF16)

    o_ref[...] = _dot(x_ref[...], w_sc[...])


def _matmul_f32w_slab(x, w, buf, *, w_lead, w_col0, row0, slab, bm, bn, vmem_mib):
    kdim = x.shape[1]
    _, rows, cols = buf.shape
    assert rows % bm == 0 and cols % bn == 0 and row0 % bm == 0 and w_col0 % bn == 0 and buf.dtype == F32
    return pl.pallas_call(
        _mm_f32w_slab_kernel,
        grid=(cols // bn, rows // bm),
        in_specs=[pl.BlockSpec((bm, kdim), lambda j, i: (row0 // bm + i, 0)),
                  pl.BlockSpec((None,) * len(w_lead) + (kdim, bn), lambda j, i: w_lead + (0, w_col0 // bn + j)),
                  pl.BlockSpec(memory_space=pl.ANY)],
        out_specs=pl.BlockSpec((None, bm, bn), lambda j, i: (slab, i, j)),
        out_shape=jax.ShapeDtypeStruct(buf.shape, F32),
        input_output_aliases={2: 0},
        scratch_shapes=[pltpu.VMEM((kdim, bn), BF16)],
        compiler_params=_params(("parallel", "arbitrary"), vmem_mib),
    )(x, w, buf)


def _cast_pad_kernel(w_ref, o_ref, *, n_real):
    @pl.when(pl.program_id(1) < n_real)
    def _():
        o_ref[...] = w_ref[...].astype(BF16)

    @pl.when(pl.program_id(1) >= n_real)
    def _():
        o_ref[...] = jnp.zeros(o_ref.shape, BF16)


def _cast_pad_rows(w, rows_pad, *, br=256):
    n, rows, cols = w.shape
    assert rows % br == 0 and rows_pad % br == 0
    n_real = rows // br
    return pl.pallas_call(
        functools.partial(_cast_pad_kernel, n_real=n_real),
        grid=(n, rows_pad // br),
        in_specs=[pl.BlockSpec((None, br, cols), lambda a, r: (a, jnp.minimum(r, n_real - 1), 0))],
        out_specs=pl.BlockSpec((None, br, cols), lambda a, r: (a, r, 0)),
        out_shape=jax.ShapeDtypeStruct((n, rows_pad, cols), BF16),
        compiler_params=_params(("parallel", "parallel"), 32),
    )(w)


def _swiglu_kernel(x_ref, wg_ref, wu_ref, o_ref, wg_sc, wu_sc, *, n_real):
    j = pl.program_id(0)

    @pl.when(jnp.logical_and(pl.program_id(1) == 0, j < n_real))
    def _():
        wg_sc[...] = wg_ref[...].astype(BF16)
        wu_sc[...] = wu_ref[...].astype(BF16)

    @pl.when(j < n_real)
    def _():
        x = x_ref[...]
        g = _dot(x, wg_sc[...])
        u = _dot(x, wu_sc[...])
        o_ref[...] = (g * jax.nn.sigmoid(g) * u).astype(o_ref.dtype)

    @pl.when(j >= n_real)
    def _():
        o_ref[...] = jnp.zeros(o_ref.shape, o_ref.dtype)


def _swiglu_in(xb, w_gu, *, w_lead, f_pad, bm, bn):
    m, kdim = xb.shape
    f = w_gu.shape[-1] // 2
    n_real = f // bn
    nj = f_pad // bn
    assert f % bn == 0 and f_pad % bn == 0 and m % bm == 0
    lead = (None,) * len(w_lead)
    col = lambda j: jnp.minimum(j, n_real - 1)
    return pl.pallas_call(
        functools.partial(_swiglu_kernel, n_real=n_real),
        grid=(nj, m // bm),
        in_specs=[pl.BlockSpec((bm, kdim), lambda j, i: (i, 0)),
                  pl.BlockSpec(lead + (kdim, bn), lambda j, i: w_lead + (0, col(j))),
                  pl.BlockSpec(lead + (kdim, bn), lambda j, i: w_lead + (0, col(j) + n_real))],
        out_specs=pl.BlockSpec((bm, bn), lambda j, i: (i, j)),
        out_shape=jax.ShapeDtypeStruct((m, f_pad), BF16),
        scratch_shapes=[pltpu.VMEM((kdim, bn), BF16), pltpu.VMEM((kdim, bn), BF16)],
        compiler_params=_params(("parallel", "arbitrary"), 48),
    )(xb, w_gu, w_gu)


def _ln_kernel(y_ref, g_ref, b_ref, o_ref, ob_ref):
    y = y_ref[...]
    mu = jnp.mean(y, axis=-1, keepdims=True)
    d = y - mu
    var = jnp.mean(d * d, axis=-1, keepdims=True)
    out = d * lax.rsqrt(var + LN_EPS) * g_ref[...] + b_ref[...]
    o_ref[...] = out
    ob_ref[...] = out.astype(BF16)


def _layer_norm(y, g, b, *, bm=256):
    m, d = y.shape
    row = pl.BlockSpec((bm, d), lambda i: (i, 0))
    vec = pl.BlockSpec((1, d), lambda i: (0, 0))
    return pl.pallas_call(
        _ln_kernel,
        grid=(m // bm,),
        in_specs=[row, vec, vec],
        out_specs=[row, row],
        out_shape=[jax.ShapeDtypeStruct((m, d), F32), jax.ShapeDtypeStruct((m, d), BF16)],
        compiler_params=_params(("parallel",), 40),
    )(y, g.reshape(1, d), b.reshape(1, d))


def _s5_kernel(u_ref, h0_ref, min_ref, minsw_ref, kseq_ref, mout_ref, dec_ref, y_ref, hfin_ref,
               s_sc, ssw_sc, hprev_sc, toep_sc, *, n_seq, n_chunk, n_single):
    ch, seq_w = kseq_ref.shape
    width = toep_sc.shape[1]
    per_tile = LANES // ch
    kseq = kseq_ref[...]
    for b in range(per_tile):
        rot = kseq if b == 0 else pltpu.roll(kseq, seq_w - ch * b, axis=1)
        for a in range(width // LANES):
            sigma = width // ch - 1 - (a * per_tile + b)
            toep_sc[sigma * ch:(sigma + 1) * ch, :] = rot[:, a * LANES:a * LANES + width]

    u = u_ref[...]
    u_hi, u_lo = _split_bf16(u)

    def dot3_u(w):
        w_hi, w_lo = _split_bf16(w)
        return _dot(u_hi, w_hi) + (_dot(u_lo, w_hi) + _dot(u_hi, w_lo))

    s_sc[...] = dot3_u(min_ref[...])
    ssw_sc[...] = dot3_u(minsw_ref[...])
    a1 = dec_ref[0:1, :]
    a2 = dec_ref[1:2, :]
    a2sw = dec_ref[2:3, :]

    h = jnp.zeros((n_seq, s_sc.shape[1]), F32)
    hsw = h
    for k in range(n_chunk):
        rows = slice(k * n_seq, (k + 1) * n_seq)
        hprev_sc[rows, :] = h
        h, hsw = (a1 * h + a2 * hsw + s_sc[rows, :], a1 * hsw + a2sw * h + ssw_sc[rows, :])
    n_chain = n_chunk * n_seq
    hfin_ref[0:n_seq, :] = h
    h0 = h0_ref[...]
    h0sw = pltpu.roll(h0, h0.shape[1] // 2, axis=1)
    hprev_sc[n_chain:n_chain + n_single, :] = h0
    hfin_ref[n_seq:n_seq + n_single, :] = a1 * h0 + a2 * h0sw + s_sc[n_chain:n_chain + n_single, :]

    y_ref[...] = dot3_u(toep_sc[...]) + _dot3(hprev_sc[...], mout_ref[...])


def _s5_scan(u_g, h0_g, mats, *, n_seq, n_chunk, n_single):
    m_in, m_in_sw, kseq, m_out, dec = mats
    g, rows, width = u_g.shape
    p2 = m_in.shape[2]
    n_out = n_seq + n_single
    grp = lambda *shape: pl.BlockSpec((None,) + shape, lambda i: (i,) + (0,) * len(shape))
    return pl.pallas_call(
        functools.partial(_s5_kernel, n_seq=n_seq, n_chunk=n_chunk, n_single=n_single),
        grid=(g,),
        in_specs=[grp(rows, width), grp(n_single, p2), grp(width, p2), grp(width, p2),
                  grp(*kseq.shape[1:]), grp(p2, width), grp(3, p2)],
        out_specs=[grp(rows, width), grp(n_out, p2)],
        out_shape=[jax.ShapeDtypeStruct((g, rows, width), F32),
                   jax.ShapeDtypeStruct((g, n_out, p2), F32)],
        scratch_shapes=[pltpu.VMEM((rows, p2), F32), pltpu.VMEM((rows, p2), F32),
                        pltpu.VMEM((rows, p2), F32), pltpu.VMEM((width, width), F32)],
        compiler_params=_params(("parallel",), 40),
    )(u_g, h0_g, m_in, m_in_sw, kseq, m_out, dec)


def _s5_matrices(lam_re, lam_im, log_step, b_re, b_im, c_re, c_im):
    hp = lax.Precision.HIGHEST
    L = S5_CHUNK
    g, p = lam_re.shape
    ch = b_re.shape[2]
    lam = lax.complex(lam_re.astype(F32), lam_im.astype(F32))
    step = jnp.exp(log_step.astype(F32))[:, None]
    lam_step = lam * step
    lam_bar = jnp.exp(lam_step)
    b_bar = ((lam_bar - 1.0) / lam)[..., None] * lax.complex(b_re.astype(F32), b_im.astype(F32))
    c_mat = lax.complex(c_re.astype(F32), c_im.astype(F32))
    d = jnp.arange(L + 1, dtype=F32)
    pw = jnp.exp(lam_step[None] * d[:, None, None])
    w_in = pw[:L][::-1].transpose(1, 0, 2)[:, :, None, :] * b_bar.transpose(0, 2, 1)[:, None, :, :]
    w_in = w_in.reshape(g, L * ch, p)
    m_in = jnp.concatenate([jnp.real(w_in), jnp.imag(w_in)], axis=-1)
    m_in_sw = jnp.concatenate([jnp.imag(w_in), jnp.real(w_in)], axis=-1)
    w_out = pw[1:].transpose(1, 2, 0)[:, :, :, None] * c_mat.transpose(0, 2, 1)[:, :, None, :]
    w_out = w_out.reshape(g, p, L * ch)
    m_out = jnp.concatenate([jnp.real(w_out), -jnp.imag(w_out)], axis=1)
    cp = c_mat[None] * pw[:L][:, :, None, :]
    kd = (jnp.einsum('dgcp,gpe->dgce', jnp.real(cp), jnp.real(b_bar), precision=hp)
          - jnp.einsum('dgcp,gpe->dgce', jnp.imag(cp), jnp.imag(b_bar), precision=hp))
    kseq = jnp.pad(kd.transpose(1, 3, 0, 2), ((0, 0), (0, 0), (L - 1, 1), (0, 0))).reshape(g, ch, 2 * L * ch)
    pl_ = pw[L]
    dec = jnp.stack([jnp.concatenate([jnp.real(pl_), jnp.real(pl_)], -1),
                     jnp.concatenate([-jnp.imag(pl_), jnp.imag(pl_)], -1),
                     jnp.concatenate([jnp.imag(pl_), -jnp.imag(pl_)], -1)], axis=1)
    return m_in, m_in_sw, kseq, m_out, dec


def _glu_kernel(y1_ref, y2_ref, u_ref, d_ref, w_ref, b_ref, o_ref, *, n_first):
    def run(y_ref):
        y = jax.nn.gelu(y_ref[...] + d_ref[...] * u_ref[...])
        z = _dot(y.astype(BF16), w_ref[...]) + b_ref[...]
        o_ref[...] = (y * jax.nn.sigmoid(z)).astype(BF16)

    pl.when(pl.program_id(0) < n_first)(lambda: run(y1_ref))
    pl.when(pl.program_id(0) >= n_first)(lambda: run(y2_ref))


def _s5_glu(y1, y2, proj, d_skip, w_glu, b_glu, *, bm=512):
    (m1, w), m2 = y1.shape, y2.shape[0]
    assert m1 % bm == 0 and m2 % bm == 0
    n_first = m1 // bm
    row = pl.BlockSpec((bm, w), lambda i: (i, 0))
    vec = pl.BlockSpec((1, w), lambda i: (0, 0))
    return pl.pallas_call(
        functools.partial(_glu_kernel, n_first=n_first),
        grid=((m1 + m2) // bm,),
        in_specs=[pl.BlockSpec((bm, w), lambda i: (jnp.minimum(i, n_first - 1), 0)),
                  pl.BlockSpec((bm, w), lambda i: (jnp.maximum(i - n_first, 0), 0)),
                  row, vec, pl.BlockSpec((w, w), lambda i: (0, 0)), vec],
        out_specs=row,
        out_shape=jax.ShapeDtypeStruct((m1 + m2, w), BF16),
        compiler_params=_params(("parallel",), 32),
    )(y1, y2, proj, d_skip.reshape(1, w), w_glu, b_glu.reshape(1, w))


def _lru_kernel(x_ref, gate_ref, h0_ref, c0_ref, cw_ref, cb_ref, wa_ref, ba_ref, wx_ref, bx_ref, sp_ref, _,
                y_ref, hout_ref, cout_ref, xpad_sc, a_sc, b_sc, hs_sc, h_sc,
                *, rows, blocks_per_seq, conv_width):
    tail = conv_width - 1
    base = 8

    @pl.when(pl.program_id(0) % blocks_per_seq == 0)
    def _():
        h_sc[...] = h0_ref[...]
        xpad_sc[base - tail:base, :] = c0_ref[...]

    x = x_ref[...]
    xpad_sc[base:base + rows, :] = x
    xc = cb_ref[...]
    for j in range(conv_width):
        off = base - tail + j
        xc = xc + cw_ref[j:j + 1, :] * xpad_sc[off:off + rows, :]
    new_tail = x_ref[rows - tail:rows, :]
    xpad_sc[base - tail:base, :] = new_tail
    cout_ref[...] = new_tail

    xcb = xc.astype(BF16)
    r = jax.nn.sigmoid(_dot(xcb, wa_ref[...]) + ba_ref[...])
    ig = jax.nn.sigmoid(_dot(xcb, wx_ref[...]) + bx_ref[...])
    log_a = -LRU_C * r * sp_ref[...]
    a = jnp.exp(log_a)
    mult = jnp.sqrt(-jnp.tanh(log_a) * (a * a + 1.0))
    a_sc[...] = a
    b_sc[...] = mult * (ig * xc)

    def step(t, h):
        h = a_sc[pl.ds(t, 1), :] * h + b_sc[pl.ds(t, 1), :]
        hs_sc[pl.ds(t, 1), :] = h
        return h

    h = lax.fori_loop(0, rows, step, h_sc[...], unroll=8)
    h_sc[...] = h
    hout_ref[...] = h
    y_ref[...] = (hs_sc[...] * jax.nn.gelu(gate_ref[...])).astype(BF16)


def _rglru(proj, h0, c0, conv_w, conv_b, wa_bd, b_a, wx_bd, b_x, sp, y_buf, *, rows, row0, blocks_per_seq,
           x_col, gate_col):
    n_seq, width = h0.shape
    cw = conv_w.shape[0]
    assert row0 % rows == 0 and y_buf.shape[1] == width and y_buf.dtype == BF16
    blk0 = row0 // rows
    vec = pl.BlockSpec((1, width), lambda i: (0, 0))
    mat = pl.BlockSpec((width, width), lambda i: (0, 0))
    hspec = pl.BlockSpec((None, 1, width), lambda i: (i // blocks_per_seq, 0, 0))
    cspec = pl.BlockSpec((None, cw - 1, width), lambda i: (i // blocks_per_seq, 0, 0))
    return pl.pallas_call(
        functools.partial(_lru_kernel, rows=rows, blocks_per_seq=blocks_per_seq, conv_width=cw),
        grid=(n_seq * blocks_per_seq,),
        in_specs=[pl.BlockSpec((rows, width), lambda i: (blk0 + i, x_col)),
                  pl.BlockSpec((rows, width), lambda i: (blk0 + i, gate_col)),
                  hspec, cspec, pl.BlockSpec((cw, width), lambda i: (0, 0)), vec, mat, vec, mat, vec, vec,
                  pl.BlockSpec(memory_space=pl.ANY)],
        out_specs=[pl.BlockSpec((rows, width), lambda i: (blk0 + i, 0)), hspec, cspec],
        out_shape=[jax.ShapeDtypeStruct(y_buf.shape, BF16),
                   jax.ShapeDtypeStruct((n_seq, 1, width), F32),
                   jax.ShapeDtypeStruct((n_seq, cw - 1, width), F32)],
        input_output_aliases={11: 0},
        scratch_shapes=[pltpu.VMEM((rows + 8, width), F32), pltpu.VMEM((rows, width), F32),
                        pltpu.VMEM((rows, width), F32), pltpu.VMEM((rows, width), F32),
                        pltpu.VMEM((1, width), F32)],
        compiler_params=_params(("arbitrary",), 32),
    )(proj, proj, h0.reshape(n_seq, 1, width), c0, conv_w, conv_b.reshape(1, width), wa_bd, b_a.reshape(1, width),
      wx_bd, b_x.reshape(1, width), sp.reshape(1, width), y_buf)


def _block_diag(w):
    n, c, d = w.shape
    eye = jnp.eye(n, dtype=w.dtype)
    return (eye[:, None, :, None] * w[:, :, None, :]).reshape(n * c, n * d)


def _head_norm(o, g, out_scale):
    o = o * lax.rsqrt(jnp.mean(o * o, axis=-1, keepdims=True) + LN_EPS) * g
    return o * out_scale


def _attn_prompt_kernel(scal_ref, sbt_ref, q_ref, k_ref, v_ref, g_ref, _, o_ref, m_sc, l_sc, acc_sc, doff_sc, ddiag_sc,
                        *, blk, hd, n_heads, out_scale):
    h = pl.program_id(1)
    i = pl.program_id(2)
    nq = pl.num_programs(2)
    slope2 = scal_ref[h]
    lam = scal_ref[n_heads]
    c1 = (hd ** -0.5) * LOG2E
    shift = int(math.log2(CHUNK))
    e = 2 * hd
    sub = blk // 4

    @pl.when(i == 0)
    def _():
        r = lax.broadcasted_iota(jnp.int32, (blk, blk), 0)
        c = lax.broadcasted_iota(jnp.int32, (blk, blk), 1)
        rel = r - c
        doff_sc[...] = slope2 * rel.astype(F32)
        visible = lax.shift_right_arithmetic(c, shift) <= lax.shift_right_arithmetic(r, shift)
        ddiag_sc[...] = jnp.where(visible, slope2 * jnp.abs(rel).astype(F32), -NEG_INF)

    q = q_ref[...].astype(BF16)
    m_sc[...] = jnp.full(m_sc.shape, NEG_INF, F32)
    l_sc[...] = jnp.zeros(l_sc.shape, F32)
    acc_sc[...] = jnp.zeros(acc_sc.shape, F32)

    def process(j, d_ref, sb):
        start = pl.multiple_of(j * blk, blk)
        kj = k_ref[pl.ds(start, blk), :].astype(BF16)
        vj = v_ref[pl.ds(start, blk), :].astype(BF16)
        for c, r in [(c, r) for r in range(blk // sub) for c in range(2)]:
            cols = slice(c * hd, (c + 1) * hd)
            rows = slice(r * sub, (r + 1) * sub)
            t = _dot_nt(q[rows, cols], kj[:, cols]) * c1 - d_ref[rows, :]
            m_prev = m_sc[c, rows, :]
            m_next = jnp.maximum(m_prev, jnp.max(t, axis=1, keepdims=True) - sb)
            p = jnp.exp2(t - jnp.tile(m_next + sb, (1, blk // LANES)))
            corr = jnp.exp2(m_prev - m_next)
            psum = p[:, 0:LANES]
            for w in range(1, blk // LANES):
                psum = psum + p[:, w * LANES:(w + 1) * LANES]
            l_sc[c, rows, :] = corr * l_sc[c, rows, :] + psum
            acc_sc[c, rows, :] = acc_sc[c, rows, :] * jnp.tile(corr, (1, e // LANES)) + _dot(p.astype(BF16), vj)
            m_sc[c, rows, :] = m_next

    def body(j, carry):
        process(j, doff_sc, sbt_ref[h * nq + (i - j)])
        return carry

    lax.fori_loop(0, i, body, 0)
    process(i, ddiag_sc, 0.0)
    outs = [acc_sc[c] / jnp.sum(l_sc[c], axis=1, keepdims=True) for c in range(2)]
    o = outs[0] - lam * outs[1]
    o_ref[...] = _head_norm(o, g_ref[...], out_scale).astype(BF16)


def _attn_prompt(proj, k_all, v_all, slopes, lam, subln, y_buf, *, layer, n_batch, seq, n_heads, hd, q_col,
                 out_scale, blk=512):
    assert blk % CHUNK == 0 and blk % LANES == 0 and seq % blk == 0
    assert y_buf.shape[1] == n_heads * 2 * hd and y_buf.dtype == BF16
    nq = seq // blk
    e = 2 * hd
    scal = jnp.concatenate([slopes * LOG2E, lam.reshape(1)])
    sb_tab = (slopes[:, None] * (LOG2E * blk * jnp.arange(nq, dtype=F32))[None, :]).reshape(-1)
    smem = pl.BlockSpec(memory_space=pltpu.SMEM)
    return pl.pallas_call(
        functools.partial(_attn_prompt_kernel, blk=blk, hd=hd, n_heads=n_heads, out_scale=out_scale),
        grid=(n_batch, n_heads, nq),
        in_specs=[smem, smem,
                  pl.BlockSpec((blk, e), lambda b, h, i: (b * nq + i, q_col + h)),
                  pl.BlockSpec((None, seq, e), lambda b, h, i: (layer, b, h)),
                  pl.BlockSpec((None, seq, e), lambda b, h, i: (layer, b, h)),
                  pl.BlockSpec((1, e), lambda b, h, i: (0, 0)), pl.BlockSpec(memory_space=pl.ANY)],
        out_specs=pl.BlockSpec((blk, e), lambda b, h, i: (b * nq + i, h)),
        out_shape=jax.ShapeDtypeStruct(y_buf.shape, BF16),
        input_output_aliases={6: 0},
        scratch_shapes=[pltpu.VMEM((2, blk, LANES), F32), pltpu.VMEM((2, blk, LANES), F32),
                        pltpu.VMEM((2, blk, e), F32), pltpu.VMEM((blk, blk), F32), pltpu.VMEM((blk, blk), F32)],
        compiler_params=_params(("parallel", "parallel", "arbitrary"), 48),
    )(scal, sb_tab, proj, k_all, v_all, subln.reshape(1, e), y_buf)


def _attn_sample_kernel(scal_ref, q_ref, kn_ref, vn_ref, g_ref, kc_hbm, vc_hbm, _, o_ref, kbuf, vbuf, sem,
                        *, layer, hd, n_heads, past, out_scale):
    b = pl.program_id(0)
    h = pl.program_id(1)
    step = b * n_heads + h
    n_steps = pl.num_programs(0) * n_heads
    slot = lax.rem(step, 2)

    def cache_copies(at_step, at_slot):
        bb = at_step // n_heads
        hh = lax.rem(at_step, n_heads)
        return (pltpu.make_async_copy(kc_hbm.at[layer, bb, :, hh, :], kbuf.at[at_slot], sem.at[0, at_slot]),
                pltpu.make_async_copy(vc_hbm.at[layer, bb, :, hh, :], vbuf.at[at_slot], sem.at[1, at_slot]))

    @pl.when(step == 0)
    def _():
        for cp in cache_copies(step, slot):
            cp.start()

    @pl.when(step + 1 < n_steps)
    def _():
        for cp in cache_copies(step + 1, 1 - slot):
            cp.start()

    slope = scal_ref[h]
    lam = scal_ref[n_heads]
    scale = hd ** -0.5
    shift = int(math.log2(CHUNK))
    t = q_ref.shape[0]
    q = q_ref[...].astype(BF16)
    kn = kn_ref[...].astype(BF16)
    vn = vn_ref[...].astype(BF16)
    qpos = past + lax.broadcasted_iota(jnp.int32, (t, 1), 0)
    bias_c = slope * (qpos - lax.broadcasted_iota(jnp.int32, (1, past), 1)).astype(F32)
    kposn = past + lax.broadcasted_iota(jnp.int32, (1, t), 1)
    bias_n = slope * jnp.abs(qpos - kposn).astype(F32)
    vis_n = lax.shift_right_arithmetic(kposn, shift) <= lax.shift_right_arithmetic(qpos, shift)

    for cp in cache_copies(step, slot):
        cp.wait()
    kc = kbuf[slot].astype(BF16)
    vc = vbuf[slot].astype(BF16)
    outs = []
    for c in range(2):
        cols = slice(c * hd, (c + 1) * hd)
        s_c = _dot_nt(q[:, cols], kc[:, cols]) * scale - bias_c
        s_n = jnp.where(vis_n, _dot_nt(q[:, cols], kn[:, cols]) * scale - bias_n, NEG_INF)
        m = jnp.maximum(jnp.max(s_c, axis=-1, keepdims=True), jnp.max(s_n, axis=-1, keepdims=True))
        p_c = jnp.exp(s_c - m)
        p_n = jnp.exp(s_n - m)
        l = jnp.sum(p_c, axis=-1, keepdims=True) + jnp.sum(p_n, axis=-1, keepdims=True)
        outs.append((_dot(p_c.astype(BF16), vc) + _dot(p_n.astype(BF16), vn)) / l)
    o = outs[0] - lam * outs[1]
    o_ref[...] = _head_norm(o, g_ref[...], out_scale).astype(BF16)


def _attn_sample(proj, k_new, v_new, cache_k, cache_v, scal, subln, y_buf, *, layer, n_batch, seq, n_heads, hd,
                 q_col, row0, out_scale):
    e = 2 * hd
    past = cache_k.shape[2]
    assert y_buf.shape[1] == n_heads * e and y_buf.dtype == BF16 and row0 % seq == 0
    rb0 = row0 // seq
    new = pl.BlockSpec((None, seq, e), lambda b, h: (layer, b, h))
    hbm = pl.BlockSpec(memory_space=pl.ANY)
    return pl.pallas_call(
        functools.partial(_attn_sample_kernel, layer=layer, hd=hd, n_heads=n_heads, past=past, out_scale=out_scale),
        grid=(n_batch, n_heads),
        in_specs=[pl.BlockSpec(memory_space=pltpu.SMEM), pl.BlockSpec((seq, e), lambda b, h: (rb0 + b, q_col + h)),
                  new, new, pl.BlockSpec((1, e), lambda b, h: (0, 0)), hbm, hbm, hbm],
        out_specs=pl.BlockSpec((seq, e), lambda b, h: (rb0 + b, h)),
        out_shape=jax.ShapeDtypeStruct(y_buf.shape, BF16),
        input_output_aliases={7: 0},
        scratch_shapes=[pltpu.VMEM((2, past, e), F32), pltpu.VMEM((2, past, e), F32),
                        pltpu.SemaphoreType.DMA((2, 2))],
        compiler_params=_params(("arbitrary", "arbitrary"), 40),
    )(scal, proj, k_new, v_new, subln.reshape(1, e), cache_k, cache_v, y_buf)


def _merge_kernel(ya_ref, yb_ref, yc_ref, wa_ref, wb_ref, wc_ref, ga_ref, gb_ref, gc_ref, bg_ref, o_ref):
    def branch(y_ref, w_ref, g_ref, r):
        return jax.nn.sigmoid(g_ref[...] + bg_ref[r:r + 1, :]) * _dot(y_ref[...], w_ref[...])

    merged = branch(ya_ref, wa_ref, ga_ref, 0) + branch(yb_ref, wb_ref, gb_ref, 1) + branch(yc_ref, wc_ref, gc_ref, 2)
    o_ref[...] = merged.astype(BF16)


def _merge(ya, yb, yc, wa, wb, wc, proj, b_gate, *, gate_col, bm=512, bn=1024):
    m = ya.shape[0]
    d = wa.shape[1]
    nj = d // bn
    y_spec = lambda y: pl.BlockSpec((bm, y.shape[1]), lambda j, i: (i, 0))
    w_spec = lambda w: pl.BlockSpec((w.shape[0], bn), lambda j, i: (0, j))
    g_spec = lambda r: pl.BlockSpec((bm, bn), lambda j, i: (i, gate_col + r * nj + j))
    return pl.pallas_call(
        _merge_kernel,
        grid=(nj, m // bm),
        in_specs=[y_spec(ya), y_spec(yb), y_spec(yc), w_spec(wa), w_spec(wb), w_spec(wc),
                  g_spec(0), g_spec(1), g_spec(2), pl.BlockSpec((3, bn), lambda j, i: (0, j))],
        out_specs=pl.BlockSpec((bm, bn), lambda j, i: (i, j)),
        out_shape=jax.ShapeDtypeStruct((m, d), BF16),
        compiler_params=_params(("parallel", "parallel"), 48),
    )(ya, yb, yc, wa, wb, wc, proj, proj, proj, b_gate)


def _ple_kernel(xb_ref, wg_ref, p_ref, wp_ref, x_ref, o_ref, ob_ref):
    gate = jax.nn.sigmoid(_dot(xb_ref[...], wg_ref[...]))
    out = x_ref[...] + gate * _dot(p_ref[...], wp_ref[...])
    o_ref[...] = out
    ob_ref[...] = out.astype(BF16)


def _ple(x, xb, pb, w_gate, w_proj, *, bm=512, bn=1024):
    m, d = x.shape
    pdim = pb.shape[1]
    tile = pl.BlockSpec((bm, bn), lambda j, i: (i, j))
    return pl.pallas_call(
        _ple_kernel,
        grid=(d // bn, m // bm),
        in_specs=[pl.BlockSpec((bm, d), lambda j, i: (i, 0)), pl.BlockSpec((d, bn), lambda j, i: (0, j)),
                  pl.BlockSpec((bm, pdim), lambda j, i: (i, 0)), pl.BlockSpec((pdim, bn), lambda j, i: (0, j)),
                  tile],
        out_specs=[tile, tile],
        out_shape=[jax.ShapeDtypeStruct((m, d), F32), jax.ShapeDtypeStruct((m, d), BF16)],
        compiler_params=_params(("parallel", "parallel"), 48),
    )(xb, w_gate, pb, w_proj, x)


def kernel(x_prompt, x_sample, cache_k, cache_v, state_s5_re, state_s5_im, state_lru, state_conv, p_prompt, p_sample, ln_g, ln_b, ffn_w_in, ffn_w_out, w_in, b_gate, s5_lam_re, s5_lam_im, s5_log_step, s5_b_re, s5_b_im, s5_c_re, s5_c_im, s5_d, s5_w_glu, s5_b_glu, diff_lambda, diff_subln, lru_conv_w, lru_conv_b, lru_w_a, lru_b_a, lru_w_x, lru_b_x, lru_lambda, w_br_a, w_br_b, w_br_c, w_o, ple_w_proj, ple_w_gate):
    bp, tp, d_model = x_prompt.shape
    bs, ts, _ = x_sample.shape
    depth = ln_g.shape[0]
    mp, ms = bp * tp, bs * ts
    m = mp + ms
    n_heads, e = cache_k.shape[3], cache_k.shape[4]
    hd = e // 2
    past = cache_k.shape[2]
    groups, p_state, grp_ch = s5_b_re.shape[1:]
    s5_w = groups * grp_ch
    att_w = n_heads * e
    lru_w = lru_lambda.shape[1]
    d_ff = ffn_w_out.shape[2]
    f_pad = -(-d_ff // 1024) * 1024
    alpha = (2 * depth) ** 0.25
    L = S5_CHUNK
    assert ts == L and tp % L == 0 and tp % LRU_ROWS_LONG == 0 and ts % 8 == 0 and ts == CHUNK and past % CHUNK == 0
    kp = tp // L
    w_k, w_v = s5_w + att_w, s5_w + 2 * att_w
    c_q = s5_w
    c_xr = s5_w + att_w
    c_gr = c_xr + lru_w
    c_gl = c_gr + lru_w
    kv_bufs = [jnp.zeros((depth, rows, att_w), F32) for rows in (mp, mp, ms, ms)]

    x = jnp.concatenate([x_prompt.reshape(mp, d_model), x_sample.reshape(ms, d_model)], axis=0)
    xb = x.astype(BF16)
    slopes = jnp.exp2(-8.0 * (jnp.arange(n_heads, dtype=F32) + 1.0) / n_heads)

    w_dn_all = _cast_pad_rows(ffn_w_out.reshape(-1, d_ff, d_model), f_pad).reshape(depth, -1, f_pad, d_model)

    def ffn(i, s, x, xb):
        hdn = _swiglu_in(xb, ffn_w_in, w_lead=(i, s), f_pad=f_pad, bm=1024, bn=256)
        y = _matmul_res(hdn, w_dn_all, x, w_lead=(i, s), bm=1024, bn=1024, bk=f_pad // 4, alpha=alpha, scale=0.5,
                        vmem_mib=48)
        return _layer_norm(y, ln_g[i, 2 * s], ln_b[i, 2 * s])

    s5r, s5i, lruh, convs = [], [], [], []
    for i in range(depth):
        x, xb = ffn(i, 0, x, xb)

        proj = _matmul_f32w(xb, w_in, w_lead=(i,), skip_cols=(w_k, w_v + att_w), bm=1024, bn=512, out_dtype=F32,
                            vmem_mib=48)
        kv_bufs = [_matmul_f32w_slab(xb, w_in, buf, w_lead=(i,), w_col0=col, row0=row0, slab=i, bm=1024, bn=512,
                                     vmem_mib=48)
                   for buf, col, row0 in zip(kv_bufs, (w_k, w_v, w_k, w_v), (0, 0, mp, mp))]
        k_p, v_p, k_s, v_s = kv_bufs

        u_p = proj[:mp, :s5_w].reshape(bp, kp, L, groups, grp_ch).transpose(3, 1, 0, 2, 4).reshape(groups, kp * bp, L * grp_ch)
        u_s = proj[mp:, :s5_w].reshape(bs, L, groups, grp_ch).transpose(2, 0, 1, 3).reshape(groups, bs, L * grp_ch)
        u_g = jnp.concatenate([u_p, u_s], axis=1)
        h0_g = jnp.concatenate([state_s5_re[i], state_s5_im[i]], axis=-1).transpose(1, 0, 2)
        mats = _s5_matrices(s5_lam_re[i], s5_lam_im[i], s5_log_step[i], s5_b_re[i], s5_b_im[i], s5_c_re[i], s5_c_im[i])
        y_g, hfin = _s5_scan(u_g, h0_g, mats, n_seq=bp, n_chunk=kp, n_single=bs)
        y_p = y_g[:, :kp * bp].reshape(groups, kp, bp, L, grp_ch).transpose(2, 1, 3, 0, 4).reshape(mp, s5_w)
        y_s = y_g[:, kp * bp:].reshape(groups, bs, L, grp_ch).transpose(1, 2, 0, 3).reshape(ms, s5_w)
        y_a = _s5_glu(y_p, y_s, proj, s5_d[i], s5_w_glu[i].astype(BF16), s5_b_glu[i])
        hfin = hfin.transpose(1, 0, 2)
        s5r.append((hfin[:bp, :, :p_state], hfin[bp:, :, :p_state]))
        s5i.append((hfin[:bp, :, p_state:], hfin[bp:, :, p_state:]))

        lam_init = 0.8 - 0.6 * math.exp(-0.3 * i)
        dl = diff_lambda[i].astype(F32)
        lam = jnp.exp(jnp.sum(dl[0] * dl[1])) - jnp.exp(jnp.sum(dl[2] * dl[3])) + lam_init
        scal = jnp.concatenate([slopes, lam.reshape(1)])
        y_b = _attn_prompt(proj, k_p, v_p, slopes, lam, diff_subln[i], jnp.zeros((m, att_w), BF16), layer=i,
                           n_batch=bp, seq=tp, n_heads=n_heads, hd=hd, out_scale=1.0 - lam_init, q_col=c_q // e)
        y_b = _attn_sample(proj, k_s, v_s, cache_k, cache_v, scal, diff_subln[i], y_b, layer=i, n_batch=bs, seq=ts,
                           n_heads=n_heads, hd=hd, row0=mp, out_scale=1.0 - lam_init, q_col=c_q // e)

        lru_params = (lru_conv_w[i], lru_conv_b[i], _block_diag(lru_w_a[i]).astype(BF16), lru_b_a[i].reshape(-1),
                      _block_diag(lru_w_x[i]).astype(BF16), lru_b_x[i].reshape(-1),
                      jax.nn.softplus(-lru_lambda[i].astype(F32)))
        lru_cols = dict(x_col=c_xr // lru_w, gate_col=c_gr // lru_w)
        y_c, h_p, c_p = _rglru(proj, jnp.zeros((bp, lru_w), F32), jnp.zeros((bp,) + state_conv.shape[2:], F32),
                               *lru_params, jnp.zeros((m, lru_w), BF16), rows=LRU_ROWS_LONG, row0=0,
                               blocks_per_seq=tp // LRU_ROWS_LONG, **lru_cols)
        y_c, h_s, c_s = _rglru(proj, state_lru[i], state_conv[i], *lru_params, y_c, rows=ts, row0=mp,
                               blocks_per_seq=1, **lru_cols)
        lruh.append((h_p[:, 0], h_s[:, 0]))
        convs.append((c_p, c_s))

        merged = _merge(y_a, y_b, y_c, w_br_a[i].astype(BF16), w_br_b[i].astype(BF16), w_br_c[i].astype(BF16),
                        proj, b_gate[i], gate_col=c_gl // 1024)
        y = _matmul_f32w(merged, w_o, x, w_lead=(i,), bm=1024, bn=512, out_dtype=F32, vmem_mib=48, alpha=alpha)
        x, xb = _layer_norm(y, ln_g[i, 1], ln_b[i, 1])

        x, xb = ffn(i, 1, x, xb)

        pb = jnp.concatenate([p_prompt[i].reshape(mp, -1), p_sample[i].reshape(ms, -1)], axis=0).astype(BF16)
        x, xb = _ple(x, xb, pb, ple_w_gate[i].astype(BF16), ple_w_proj[i].astype(BF16))


    stack = lambda pairs, which: jnp.stack([pr[which] for pr in pairs])
    return (x[:mp].reshape(bp, tp, d_model), x[mp:].reshape(bs, ts, d_model),
            k_p.reshape(depth, bp, tp, n_heads, e), v_p.reshape(depth, bp, tp, n_heads, e),
            stack(s5r, 0), stack(s5i, 0), stack(lruh, 0), stack(convs, 0),
            k_s.reshape(depth, bs, ts, n_heads, e), v_s.reshape(depth, bs, ts, n_heads, e),
            stack(s5r, 1), stack(s5i, 1), stack(lruh, 1), stack(convs, 1))
```

```python
import functools
import math

import jax
import jax.numpy as jnp
from jax import lax
from jax.experimental import pallas as pl
from jax.experimental.pallas import tpu as pltpu

F32 = jnp.float32
BF16 = jnp.bfloat16

CHUNK = 64
LRU_C = 8.0
LN_EPS = 1e-5
NEG_INF = -1e30
S5_CHUNK = 64
LRU_ROWS_LONG = 256

LANES = 128
LOG2E = math.log2(math.e)

MIB = 1024 * 1024


def _params(semantics, vmem_mib):
    return pltpu.CompilerParams(dimension_semantics=semantics, vmem_limit_bytes=vmem_mib * MIB)


def _dot(a, b):
    return jnp.dot(a, b, preferred_element_type=F32)


def _dot_nt(a, b):
    return lax.dot_general(a, b, (((1,), (1,)), ((), ())), preferred_element_type=F32)


def _split_bf16(a):
    hi = a.astype(BF16)
    lo = (a - hi.astype(F32)).astype(BF16)
    return hi, lo


def _dot3(a, b):
    a_hi, a_lo = _split_bf16(a)
    b_hi, b_lo = _split_bf16(b)
    return _dot(a_hi, b_hi) + (_dot(a_lo, b_hi) + _dot(a_hi, b_lo))


def _mm_res_kernel(x_ref, w_ref, r_ref, o_ref, *, nk, alpha, scale):
    part = _dot(x_ref[...], w_ref[...])
    k = pl.program_id(2)

    @pl.when(k == 0)
    def _():
        o_ref[...] = part

    @pl.when(k > 0)
    def _():
        o_ref[...] += part

    @pl.when(k == nk - 1)
    def _():
        o_ref[...] = alpha * r_ref[...] + scale * o_ref[...]


def _matmul_res(x, w, res, *, w_lead=(), bm, bn, bk, alpha, scale, vmem_mib):
    m, kdim = x.shape
    n = w.shape[-1]
    nk = kdim // bk
    assert w.shape[-2] == kdim and m % bm == 0 and n % bn == 0 and kdim % bk == 0 and nk > 1
    tile = pl.BlockSpec((bm, bn), lambda i, j, k: (i, j))
    return pl.pallas_call(
        functools.partial(_mm_res_kernel, nk=nk, alpha=alpha, scale=scale),
        grid=(m // bm, n // bn, nk),
        in_specs=[pl.BlockSpec((bm, bk), lambda i, j, k: (i, k)),
                  pl.BlockSpec((None,) * len(w_lead) + (bk, bn), lambda i, j, k: w_lead + (k, j)), tile],
        out_specs=tile,
        out_shape=jax.ShapeDtypeStruct((m, n), F32),
        compiler_params=_params(("parallel", "parallel", "arbitrary"), vmem_mib),
    )(x, w, res)


def _mm_f32w_kernel(x_ref, w_ref, *rest, alpha, scale):
    o_ref, w_sc = rest[-2:]

    @pl.when(pl.program_id(1) == 0)
    def _():
        w_sc[...] = w_ref[...].astype(BF16)

    acc = _dot(x_ref[...], w_sc[...])
    if len(rest) == 3:
        acc = alpha * rest[0][...] + scale * acc
    o_ref[...] = acc.astype(o_ref.dtype)


def _matmul_f32w(x, w, res=None, *, w_lead=(), skip_cols=(0, 0), bm, bn, out_dtype, vmem_mib, alpha=1.0, scale=1.0):
    m, kdim = x.shape
    lo, hi = skip_cols[0] // bn, skip_cols[1] // bn
    n = w.shape[-1] - (hi - lo) * bn
    assert w.shape[-2] == kdim and m % bm == 0 and n % bn == 0 and skip_cols == (lo * bn, hi * bn)
    w_col = lambda j: j + (hi - lo) * (j >= lo).astype(jnp.int32) if hi > lo else j
    tile = pl.BlockSpec((bm, bn), lambda j, i: (i, j))
    extra = [] if res is None else [res]
    return pl.pallas_call(
        functools.partial(_mm_f32w_kernel, alpha=alpha, scale=scale),
        grid=(n // bn, m // bm),
        in_specs=[pl.BlockSpec((bm, kdim), lambda j, i: (i, 0)),
                  pl.BlockSpec((None,) * len(w_lead) + (kdim, bn), lambda j, i: w_lead + (0, w_col(j)))]
                 + [tile] * len(extra),
        out_specs=tile,
        out_shape=jax.ShapeDtypeStruct((m, n), out_dtype),
        scratch_shapes=[pltpu.VMEM((kdim, bn), BF16)],
        compiler_params=_params(("parallel", "arbitrary"), vmem_mib),
    )(x, w, *extra)


def _mm_f32w_slab_kernel(x_ref, w_ref, _, o_ref, w_sc):
    @pl.when(pl.program_id(1) == 0)
    def _():
        w_sc[...] = w_ref[...].astype(BF16)

    o_ref[...] = _dot(x_ref[...], w_sc[...])


def _matmul_f32w_slab(x, w, buf, *, w_lead, w_col0, row0, slab, bm, bn, vmem_mib):
    kdim = x.shape[1]
    _, rows, cols = buf.shape
    assert rows % bm == 0 and cols % bn == 0 and row0 % bm == 0 and w_col0 % bn == 0 and buf.dtype == F32
    return pl.pallas_call(
        _mm_f32w_slab_kernel,
        grid=(cols // bn, rows // bm),
        in_specs=[pl.BlockSpec((bm, kdim), lambda j, i: (row0 // bm + i, 0)),
                  pl.BlockSpec((None,) * len(w_lead) + (kdim, bn), lambda j, i: w_lead + (0, w_col0 // bn + j)),
                  pl.BlockSpec(memory_space=pl.ANY)],
        out_specs=pl.BlockSpec((None, bm, bn), lambda j, i: (slab, i, j)),
        out_shape=jax.ShapeDtypeStruct(buf.shape, F32),
        input_output_aliases={2: 0},
        scratch_shapes=[pltpu.VMEM((kdim, bn), BF16)],
        compiler_params=_params(("parallel", "arbitrary"), vmem_mib),
    )(x, w, buf)


def _cast_pad_kernel(w_ref, o_ref, *, n_real):
    @pl.when(pl.program_id(1) < n_real)
    def _():
        o_ref[...] = w_ref[...].astype(BF16)

    @pl.when(pl.program_id(1) >= n_real)
    def _():
        o_ref[...] = jnp.zeros(o_ref.shape, BF16)


def _cast_pad_rows(w, rows_pad, *, br=256):
    n, rows, cols = w.shape
    assert rows % br == 0 and rows_pad % br == 0
    n_real = rows // br
    return pl.pallas_call(
        functools.partial(_cast_pad_kernel, n_real=n_real),
        grid=(n, rows_pad // br),
        in_specs=[pl.BlockSpec((None, br, cols), lambda a, r: (a, jnp.minimum(r, n_real - 1), 0))],
        out_specs=pl.BlockSpec((None, br, cols), lambda a, r: (a, r, 0)),
        out_shape=jax.ShapeDtypeStruct((n, rows_pad, cols), BF16),
        compiler_params=_params(("parallel", "parallel"), 32),
    )(w)


def _swiglu_kernel(x_ref, wg_ref, wu_ref, o_ref, wg_sc, wu_sc, *, n_real):
    j = pl.program_id(0)

    @pl.when(jnp.logical_and(pl.program_id(1) == 0, j < n_real))
    def _():
        wg_sc[...] = wg_ref[...].astype(BF16)
        wu_sc[...] = wu_ref[...].astype(BF16)

    @pl.when(j < n_real)
    def _():
        x = x_ref[...]
        g = _dot(x, wg_sc[...])
        u = _dot(x, wu_sc[...])
        o_ref[...] = (g * jax.nn.sigmoid(g) * u).astype(o_ref.dtype)

    @pl.when(j >= n_real)
    def _():
        o_ref[...] = jnp.zeros(o_ref.shape, o_ref.dtype)


def _swiglu_in(xb, w_gu, *, w_lead, f_pad, bm, bn):
    m, kdim = xb.shape
    f = w_gu.shape[-1] // 2
    n_real = f // bn
    nj = f_pad // bn
    assert f % bn == 0 and f_pad % bn == 0 and m % bm == 0
    lead = (None,) * len(w_lead)
    col = lambda j: jnp.minimum(j, n_real - 1)
    return pl.pallas_call(
        functools.partial(_swiglu_kernel, n_real=n_real),
        grid=(nj, m // bm),
        in_specs=[pl.BlockSpec((bm, kdim), lambda j, i: (i, 0)),
                  pl.BlockSpec(lead + (kdim, bn), lambda j, i: w_lead + (0, col(j))),
                  pl.BlockSpec(lead + (kdim, bn), lambda j, i: w_lead + (0, col(j) + n_real))],
        out_specs=pl.BlockSpec((bm, bn), lambda j, i: (i, j)),
        out_shape=jax.ShapeDtypeStruct((m, f_pad), BF16),
        scratch_shapes=[pltpu.VMEM((kdim, bn), BF16), pltpu.VMEM((kdim, bn), BF16)],
        compiler_params=_params(("parallel", "arbitrary"), 48),
    )(xb, w_gu, w_gu)


def _ln_kernel(y_ref, g_ref, b_ref, o_ref, ob_ref):
    y = y_ref[...]
    mu = jnp.mean(y, axis=-1, keepdims=True)
    d = y - mu
    var = jnp.mean(d * d, axis=-1, keepdims=True)
    out = d * lax.rsqrt(var + LN_EPS) * g_ref[...] + b_ref[...]
    o_ref[...] = out
    ob_ref[...] = out.astype(BF16)


def _layer_norm(y, g, b, *, bm=256):
    m, d = y.shape
    row = pl.BlockSpec((bm, d), lambda i: (i, 0))
    vec = pl.BlockSpec((1, d), lambda i: (0, 0))
    return pl.pallas_call(
        _ln_kernel,
        grid=(m // bm,),
        in_specs=[row, vec, vec],
        out_specs=[row, row],
        out_shape=[jax.ShapeDtypeStruct((m, d), F32), jax.ShapeDtypeStruct((m, d), BF16)],
        compiler_params=_params(("parallel",), 40),
    )(y, g.reshape(1, d), b.reshape(1, d))


def _s5_kernel(u_ref, h0_ref, min_ref, minsw_ref, kseq_ref, mout_ref, dec_ref, y_ref, hfin_ref,
               s_sc, ssw_sc, hprev_sc, toep_sc, *, n_seq, n_chunk, n_single):
    ch, seq_w = kseq_ref.shape
    width = toep_sc.shape[1]
    per_tile = LANES // ch
    kseq = kseq_ref[...]
    for b in range(per_tile):
        rot = kseq if b == 0 else pltpu.roll(kseq, seq_w - ch * b, axis=1)
        for a in range(width // LANES):
            sigma = width // ch - 1 - (a * per_tile + b)
            toep_sc[sigma * ch:(sigma + 1) * ch, :] = rot[:, a * LANES:a * LANES + width]

    u = u_ref[...]
    u_hi, u_lo = _split_bf16(u)

    def dot3_u(w):
        w_hi, w_lo = _split_bf16(w)
        return _dot(u_hi, w_hi) + (_dot(u_lo, w_hi) + _dot(u_hi, w_lo))

    s_sc[...] = dot3_u(min_ref[...])
    ssw_sc[...] = dot3_u(minsw_ref[...])
    a1 = dec_ref[0:1, :]
    a2 = dec_ref[1:2, :]
    a2sw = dec_ref[2:3, :]

    h = jnp.zeros((n_seq, s_sc.shape[1]), F32)
    hsw = h
    for k in range(n_chunk):
        rows = slice(k * n_seq, (k + 1) * n_seq)
        hprev_sc[rows, :] = h
        h, hsw = (a1 * h + a2 * hsw + s_sc[rows, :], a1 * hsw + a2sw * h + ssw_sc[rows, :])
    n_chain = n_chunk * n_seq
    hfin_ref[0:n_seq, :] = h
    h0 = h0_ref[...]
    h0sw = pltpu.roll(h0, h0.shape[1] // 2, axis=1)
    hprev_sc[n_chain:n_chain + n_single, :] = h0
    hfin_ref[n_seq:n_seq + n_single, :] = a1 * h0 + a2 * h0sw + s_sc[n_chain:n_chain + n_single, :]

    y_ref[...] = dot3_u(toep_sc[...]) + _dot3(hprev_sc[...], mout_ref[...])


def _s5_scan(u_g, h0_g, mats, *, n_seq, n_chunk, n_single):
    m_in, m_in_sw, kseq, m_out, dec = mats
    g, rows, width = u_g.shape
    p2 = m_in.shape[2]
    n_out = n_seq + n_single
    grp = lambda *shape: pl.BlockSpec((None,) + shape, lambda i: (i,) + (0,) * len(shape))
    return pl.pallas_call(
        functools.partial(_s5_kernel, n_seq=n_seq, n_chunk=n_chunk, n_single=n_single),
        grid=(g,),
        in_specs=[grp(rows, width), grp(n_single, p2), grp(width, p2), grp(width, p2),
                  grp(*kseq.shape[1:]), grp(p2, width), grp(3, p2)],
        out_specs=[grp(rows, width), grp(n_out, p2)],
        out_shape=[jax.ShapeDtypeStruct((g, rows, width), F32),
                   jax.ShapeDtypeStruct((g, n_out, p2), F32)],
        scratch_shapes=[pltpu.VMEM((rows, p2), F32), pltpu.VMEM((rows, p2), F32),
                        pltpu.VMEM((rows, p2), F32), pltpu.VMEM((width, width), F32)],
        compiler_params=_params(("parallel",), 40),
    )(u_g, h0_g, m_in, m_in_sw, kseq, m_out, dec)


def _s5_matrices(lam_re, lam_im, log_step, b_re, b_im, c_re, c_im):
    hp = lax.Precision.HIGHEST
    L = S5_CHUNK
    g, p = lam_re.shape
    ch = b_re.shape[2]
    lam = lax.complex(lam_re.astype(F32), lam_im.astype(F32))
    step = jnp.exp(log_step.astype(F32))[:, None]
    lam_step = lam * step
    lam_bar = jnp.exp(lam_step)
    b_bar = ((lam_bar - 1.0) / lam)[..., None] * lax.complex(b_re.astype(F32), b_im.astype(F32))
    c_mat = lax.complex(c_re.astype(F32), c_im.astype(F32))
    d = jnp.arange(L + 1, dtype=F32)
    pw = jnp.exp(lam_step[None] * d[:, None, None])
    w_in = pw[:L][::-1].transpose(1, 0, 2)[:, :, None, :] * b_bar.transpose(0, 2, 1)[:, None, :, :]
    w_in = w_in.reshape(g, L * ch, p)
    m_in = jnp.concatenate([jnp.real(w_in), jnp.imag(w_in)], axis=-1)
    m_in_sw = jnp.concatenate([jnp.imag(w_in), jnp.real(w_in)], axis=-1)
    w_out = pw[1:].transpose(1, 2, 0)[:, :, :, None] * c_mat.transpose(0, 2, 1)[:, :, None, :]
    w_out = w_out.reshape(g, p, L * ch)
    m_out = jnp.concatenate([jnp.real(w_out), -jnp.imag(w_out)], axis=1)
    cp = c_mat[None] * pw[:L][:, :, None, :]
    kd = (jnp.einsum('dgcp,gpe->dgce', jnp.real(cp), jnp.real(b_bar), precision=hp)
          - jnp.einsum('dgcp,gpe->dgce', jnp.imag(cp), jnp.imag(b_bar), precision=hp))
    kseq = jnp.pad(kd.transpose(1, 3, 0, 2), ((0, 0), (0, 0), (L - 1, 1), (0, 0))).reshape(g, ch, 2 * L * ch)
    pl_ = pw[L]
    dec = jnp.stack([jnp.concatenate([jnp.real(pl_), jnp.real(pl_)], -1),
                     jnp.concatenate([-jnp.imag(pl_), jnp.imag(pl_)], -1),
                     jnp.concatenate([jnp.imag(pl_), -jnp.imag(pl_)], -1)], axis=1)
    return m_in, m_in_sw, kseq, m_out, dec


def _glu_kernel(y1_ref, y2_ref, u_ref, d_ref, w_ref, b_ref, o_ref, *, n_first):
    def run(y_ref):
        y = jax.nn.gelu(y_ref[...] + d_ref[...] * u_ref[...])
        z = _dot(y.astype(BF16), w_ref[...]) + b_ref[...]
        o_ref[...] = (y * jax.nn.sigmoid(z)).astype(BF16)

    pl.when(pl.program_id(0) < n_first)(lambda: run(y1_ref))
    pl.when(pl.program_id(0) >= n_first)(lambda: run(y2_ref))


def _s5_glu(y1, y2, proj, d_skip, w_glu, b_glu, *, bm=512):
    (m1, w), m2 = y1.shape, y2.shape[0]
    assert m1 % bm == 0 and m2 % bm == 0
    n_first = m1 // bm
    row = pl.BlockSpec((bm, w), lambda i: (i, 0))
    vec = pl.BlockSpec((1, w), lambda i: (0, 0))
    return pl.pallas_call(
        functools.partial(_glu_kernel, n_first=n_first),
        grid=((m1 + m2) // bm,),
        in_specs=[pl.BlockSpec((bm, w), lambda i: (jnp.minimum(i, n_first - 1), 0)),
                  pl.BlockSpec((bm, w), lambda i: (jnp.maximum(i - n_first, 0), 0)),
                  row, vec, pl.BlockSpec((w, w), lambda i: (0, 0)), vec],
        out_specs=row,
        out_shape=jax.ShapeDtypeStruct((m1 + m2, w), BF16),
        compiler_params=_params(("parallel",), 32),
    )(y1, y2, proj, d_skip.reshape(1, w), w_glu, b_glu.reshape(1, w))


def _lru_kernel(x_ref, gate_ref, h0_ref, c0_ref, cw_ref, cb_ref, wa_ref, ba_ref, wx_ref, bx_ref, sp_ref, _,
                y_ref, hout_ref, cout_ref, xpad_sc, a_sc, b_sc, hs_sc, h_sc,
                *, rows, blocks_per_seq, conv_width):
    tail = conv_width - 1
    base = 8

    @pl.when(pl.program_id(0) % blocks_per_seq == 0)
    def _():
        h_sc[...] = h0_ref[...]
        xpad_sc[base - tail:base, :] = c0_ref[...]

    x = x_ref[...]
    xpad_sc[base:base + rows, :] = x
    xc = cb_ref[...]
    for j in range(conv_width):
        off = base - tail + j
        xc = xc + cw_ref[j:j + 1, :] * xpad_sc[off:off + rows, :]
    new_tail = x_ref[rows - tail:rows, :]
    xpad_sc[base - tail:base, :] = new_tail
    cout_ref[...] = new_tail

    xcb = xc.astype(BF16)
    r = jax.nn.sigmoid(_dot(xcb, wa_ref[...]) + ba_ref[...])
    ig = jax.nn.sigmoid(_dot(xcb, wx_ref[...]) + bx_ref[...])
    log_a = -LRU_C * r * sp_ref[...]
    a = jnp.exp(log_a)
    mult = jnp.sqrt(-jnp.tanh(log_a) * (a * a + 1.0))
    a_sc[...] = a
    b_sc[...] = mult * (ig * xc)

    def step(t, h):
        h = a_sc[pl.ds(t, 1), :] * h + b_sc[pl.ds(t, 1), :]
        hs_sc[pl.ds(t, 1), :] = h
        return h

    h = lax.fori_loop(0, rows, step, h_sc[...], unroll=8)
    h_sc[...] = h
    hout_ref[...] = h
    y_ref[...] = (hs_sc[...] * jax.nn.gelu(gate_ref[...])).astype(BF16)


def _rglru(proj, h0, c0, conv_w, conv_b, wa_bd, b_a, wx_bd, b_x, sp, y_buf, *, rows, row0, blocks_per_seq,
           x_col, gate_col):
    n_seq, width = h0.shape
    cw = conv_w.shape[0]
    assert row0 % rows == 0 and y_buf.shape[1] == width and y_buf.dtype == BF16
    blk0 = row0 // rows
    vec = pl.BlockSpec((1, width), lambda i: (0, 0))
    mat = pl.BlockSpec((width, width), lambda i: (0, 0))
    hspec = pl.BlockSpec((None, 1, width), lambda i: (i // blocks_per_seq, 0, 0))
    cspec = pl.BlockSpec((None, cw - 1, width), lambda i: (i // blocks_per_seq, 0, 0))
    return pl.pallas_call(
        functools.partial(_lru_kernel, rows=rows, blocks_per_seq=blocks_per_seq, conv_width=cw),
        grid=(n_seq * blocks_per_seq,),
        in_specs=[pl.BlockSpec((rows, width), lambda i: (blk0 + i, x_col)),
                  pl.BlockSpec((rows, width), lambda i: (blk0 + i, gate_col)),
                  hspec, cspec, pl.BlockSpec((cw, width), lambda i: (0, 0)), vec, mat, vec, mat, vec, vec,
                  pl.BlockSpec(memory_space=pl.ANY)],
        out_specs=[pl.BlockSpec((rows, width), lambda i: (blk0 + i, 0)), hspec, cspec],
        out_shape=[jax.ShapeDtypeStruct(y_buf.shape, BF16),
                   jax.ShapeDtypeStruct((n_seq, 1, width), F32),
                   jax.ShapeDtypeStruct((n_seq, cw - 1, width), F32)],
        input_output_aliases={11: 0},
        scratch_shapes=[pltpu.VMEM((rows + 8, width), F32), pltpu.VMEM((rows, width), F32),
                        pltpu.VMEM((rows, width), F32), pltpu.VMEM((rows, width), F32),
                        pltpu.VMEM((1, width), F32)],
        compiler_params=_params(("arbitrary",), 32),
    )(proj, proj, h0.reshape(n_seq, 1, width), c0, conv_w, conv_b.reshape(1, width), wa_bd, b_a.reshape(1, width),
      wx_bd, b_x.reshape(1, width), sp.reshape(1, width), y_buf)


def _block_diag(w):
    n, c, d = w.shape
    eye = jnp.eye(n, dtype=w.dtype)
    return (eye[:, None, :, None] * w[:, :, None, :]).reshape(n * c, n * d)


def _head_norm(o, g, out_scale):
    o = o * lax.rsqrt(jnp.mean(o * o, axis=-1, keepdims=True) + LN_EPS) * g
    return o * out_scale


def _attn_prompt_kernel(scal_ref, sbt_ref, q_ref, k_ref, v_ref, g_ref, _, o_ref, m_sc, l_sc, acc_sc, doff_sc, ddiag_sc,
                        *, blk, hd, n_heads, out_scale):
    h = pl.program_id(1)
    i = pl.program_id(2)
    nq = pl.num_programs(2)
    slope2 = scal_ref[h]
    lam = scal_ref[n_heads]
    c1 = (hd ** -0.5) * LOG2E
    shift = int(math.log2(CHUNK))
    e = 2 * hd
    sub = blk // 4

    @pl.when(i == 0)
    def _():
        r = lax.broadcasted_iota(jnp.int32, (blk, blk), 0)
        c = lax.broadcasted_iota(jnp.int32, (blk, blk), 1)
        rel = r - c
        doff_sc[...] = slope2 * rel.astype(F32)
        visible = lax.shift_right_arithmetic(c, shift) <= lax.shift_right_arithmetic(r, shift)
        ddiag_sc[...] = jnp.where(visible, slope2 * jnp.abs(rel).astype(F32), -NEG_INF)

    q = q_ref[...].astype(BF16)
    m_sc[...] = jnp.full(m_sc.shape, NEG_INF, F32)
    l_sc[...] = jnp.zeros(l_sc.shape, F32)
    acc_sc[...] = jnp.zeros(acc_sc.shape, F32)

    def process(j, d_ref, sb):
        start = pl.multiple_of(j * blk, blk)
        kj = k_ref[pl.ds(start, blk), :].astype(BF16)
        vj = v_ref[pl.ds(start, blk), :].astype(BF16)
        for c, r in [(c, r) for r in range(blk // sub) for c in range(2)]:
            cols = slice(c * hd, (c + 1) * hd)
            rows = slice(r * sub, (r + 1) * sub)
            t = _dot_nt(q[rows, cols], kj[:, cols]) * c1 - d_ref[rows, :]
            m_prev = m_sc[c, rows, :]
            m_next = jnp.maximum(m_prev, jnp.max(t, axis=1, keepdims=True) - sb)
            p = jnp.exp2(t - jnp.tile(m_next + sb, (1, blk // LANES)))
            corr = jnp.exp2(m_prev - m_next)
            psum = p[:, 0:LANES]
            for w in range(1, blk // LANES):
                psum = psum + p[:, w * LANES:(w + 1) * LANES]
            l_sc[c, rows, :] = corr * l_sc[c, rows, :] + psum
            acc_sc[c, rows, :] = acc_sc[c, rows, :] * jnp.tile(corr, (1, e // LANES)) + _dot(p.astype(BF16), vj)
            m_sc[c, rows, :] = m_next

    def body(j, carry):
        process(j, doff_sc, sbt_ref[h * nq + (i - j)])
        return carry

    lax.fori_loop(0, i, body, 0)
    process(i, ddiag_sc, 0.0)
    outs = [acc_sc[c] / jnp.sum(l_sc[c], axis=1, keepdims=True) for c in range(2)]
    o = outs[0] - lam * outs[1]
    o_ref[...] = _head_norm(o, g_ref[...], out_scale).astype(BF16)


def _attn_prompt(proj, k_all, v_all, slopes, lam, subln, y_buf, *, layer, n_batch, seq, n_heads, hd, q_col,
                 out_scale, blk=512):
    assert blk % CHUNK == 0 and blk % LANES == 0 and seq % blk == 0
    assert y_buf.shape[1] == n_heads * 2 * hd and y_buf.dtype == BF16
    nq = seq // blk
    e = 2 * hd
    scal = jnp.concatenate([slopes * LOG2E, lam.reshape(1)])
    sb_tab = (slopes[:, None] * (LOG2E * blk * jnp.arange(nq, dtype=F32))[None, :]).reshape(-1)
    smem = pl.BlockSpec(memory_space=pltpu.SMEM)
    return pl.pallas_call(
        functools.partial(_attn_prompt_kernel, blk=blk, hd=hd, n_heads=n_heads, out_scale=out_scale),
        grid=(n_batch, n_heads, nq),
        in_specs=[smem, smem,
                  pl.BlockSpec((blk, e), lambda b, h, i: (b * nq + i, q_col + h)),
                  pl.BlockSpec((None, seq, e), lambda b, h, i: (layer, b, h)),
                  pl.BlockSpec((None, seq, e), lambda b, h, i: (layer, b, h)),
                  pl.BlockSpec((1, e), lambda b, h, i: (0, 0)), pl.BlockSpec(memory_space=pl.ANY)],
        out_specs=pl.BlockSpec((blk, e), lambda b, h, i: (b * nq + i, h)),
        out_shape=jax.ShapeDtypeStruct(y_buf.shape, BF16),
        input_output_aliases={6: 0},
        scratch_shapes=[pltpu.VMEM((2, blk, LANES), F32), pltpu.VMEM((2, blk, LANES), F32),
                        pltpu.VMEM((2, blk, e), F32), pltpu.VMEM((blk, blk), F32), pltpu.VMEM((blk, blk), F32)],
        compiler_params=_params(("parallel", "parallel", "arbitrary"), 48),
    )(scal, sb_tab, proj, k_all, v_all, subln.reshape(1, e), y_buf)


def _attn_sample_kernel(scal_ref, q_ref, kn_ref, vn_ref, g_ref, kc_hbm, vc_hbm, _, o_ref, kbuf, vbuf, sem,
                        *, layer, hd, n_heads, past, out_scale):
    b = pl.program_id(0)
    h = pl.program_id(1)
    step = b * n_heads + h
    n_steps = pl.num_programs(0) * n_heads
    slot = lax.rem(step, 2)

    def cache_copies(at_step, at_slot):
        bb = at_step // n_heads
        hh = lax.rem(at_step, n_heads)
        return (pltpu.make_async_copy(kc_hbm.at[layer, bb, :, hh, :], kbuf.at[at_slot], sem.at[0, at_slot]),
                pltpu.make_async_copy(vc_hbm.at[layer, bb, :, hh, :], vbuf.at[at_slot], sem.at[1, at_slot]))

    @pl.when(step == 0)
    def _():
        for cp in cache_copies(step, slot):
            cp.start()

    @pl.when(step + 1 < n_steps)
    def _():
        for cp in cache_copies(step + 1, 1 - slot):
            cp.start()

    slope = scal_ref[h]
    lam = scal_ref[n_heads]
    scale = hd ** -0.5
    shift = int(math.log2(CHUNK))
    t = q_ref.shape[0]
    q = q_ref[...].astype(BF16)
    kn = kn_ref[...].astype(BF16)
    vn = vn_ref[...].astype(BF16)
    qpos = past + lax.broadcasted_iota(jnp.int32, (t, 1), 0)
    bias_c = slope * (qpos - lax.broadcasted_iota(jnp.int32, (1, past), 1)).astype(F32)
    kposn = past + lax.broadcasted_iota(jnp.int32, (1, t), 1)
    bias_n = slope * jnp.abs(qpos - kposn).astype(F32)
    vis_n = lax.shift_right_arithmetic(kposn, shift) <= lax.shift_right_arithmetic(qpos, shift)

    for cp in cache_copies(step, slot):
        cp.wait()
    kc = kbuf[slot].astype(BF16)
    vc = vbuf[slot].astype(BF16)
    outs = []
    for c in range(2):
        cols = slice(c * hd, (c + 1) * hd)
        s_c = _dot_nt(q[:, cols], kc[:, cols]) * scale - bias_c
        s_n = jnp.where(vis_n, _dot_nt(q[:, cols], kn[:, cols]) * scale - bias_n, NEG_INF)
        m = jnp.maximum(jnp.max(s_c, axis=-1, keepdims=True), jnp.max(s_n, axis=-1, keepdims=True))
        p_c = jnp.exp(s_c - m)
        p_n = jnp.exp(s_n - m)
        l = jnp.sum(p_c, axis=-1, keepdims=True) + jnp.sum(p_n, axis=-1, keepdims=True)
        outs.append((_dot(p_c.astype(BF16), vc) + _dot(p_n.astype(BF16), vn)) / l)
    o = outs[0] - lam * outs[1]
    o_ref[...] = _head_norm(o, g_ref[...], out_scale).astype(BF16)


def _attn_sample(proj, k_new, v_new, cache_k, cache_v, scal, subln, y_buf, *, layer, n_batch, seq, n_heads, hd,
                 q_col, row0, out_scale):
    e = 2 * hd
    past = cache_k.shape[2]
    assert y_buf.shape[1] == n_heads * e and y_buf.dtype == BF16 and row0 % seq == 0
    rb0 = row0 // seq
    new = pl.BlockSpec((None, seq, e), lambda b, h: (layer, b, h))
    hbm = pl.BlockSpec(memory_space=pl.ANY)
    return pl.pallas_call(
        functools.partial(_attn_sample_kernel, layer=layer, hd=hd, n_heads=n_heads, past=past, out_scale=out_scale),
        grid=(n_batch, n_heads),
        in_specs=[pl.BlockSpec(memory_space=pltpu.SMEM), pl.BlockSpec((seq, e), lambda b, h: (rb0 + b, q_col + h)),
                  new, new, pl.BlockSpec((1, e), lambda b, h: (0, 0)), hbm, hbm, hbm],
        out_specs=pl.BlockSpec((seq, e), lambda b, h: (rb0 + b, h)),
        out_shape=jax.ShapeDtypeStruct(y_buf.shape, BF16),
        input_output_aliases={7: 0},
        scratch_shapes=[pltpu.VMEM((2, past, e), F32), pltpu.VMEM((2, past, e), F32),
                        pltpu.SemaphoreType.DMA((2, 2))],
        compiler_params=_params(("arbitrary", "arbitrary"), 40),
    )(scal, proj, k_new, v_new, subln.reshape(1, e), cache_k, cache_v, y_buf)


def _merge_kernel(ya_ref, yb_ref, yc_ref, wa_ref, wb_ref, wc_ref, ga_ref, gb_ref, gc_ref, bg_ref, o_ref):
    def branch(y_ref, w_ref, g_ref, r):
        return jax.nn.sigmoid(g_ref[...] + bg_ref[r:r + 1, :]) * _dot(y_ref[...], w_ref[...])

    merged = branch(ya_ref, wa_ref, ga_ref, 0) + branch(yb_ref, wb_ref, gb_ref, 1) + branch(yc_ref, wc_ref, gc_ref, 2)
    o_ref[...] = merged.astype(BF16)


def _merge(ya, yb, yc, wa, wb, wc, proj, b_gate, *, gate_col, bm=512, bn=1024):
    m = ya.shape[0]
    d = wa.shape[1]
    nj = d // bn
    y_spec = lambda y: pl.BlockSpec((bm, y.shape[1]), lambda j, i: (i, 0))
    w_spec = lambda w: pl.BlockSpec((w.shape[0], bn), lambda j, i: (0, j))
    g_spec = lambda r: pl.BlockSpec((bm, bn), lambda j, i: (i, gate_col + r * nj + j))
    return pl.pallas_call(
        _merge_kernel,
        grid=(nj, m // bm),
        in_specs=[y_spec(ya), y_spec(yb), y_spec(yc), w_spec(wa), w_spec(wb), w_spec(wc),
                  g_spec(0), g_spec(1), g_spec(2), pl.BlockSpec((3, bn), lambda j, i: (0, j))],
        out_specs=pl.BlockSpec((bm, bn), lambda j, i: (i, j)),
        out_shape=jax.ShapeDtypeStruct((m, d), BF16),
        compiler_params=_params(("parallel", "parallel"), 48),
    )(ya, yb, yc, wa, wb, wc, proj, proj, proj, b_gate)


def _ple_kernel(xb_ref, wg_ref, p_ref, wp_ref, x_ref, o_ref, ob_ref):
    gate = jax.nn.sigmoid(_dot(xb_ref[...], wg_ref[...]))
    out = x_ref[...] + gate * _dot(p_ref[...], wp_ref[...])
    o_ref[...] = out
    ob_ref[...] = out.astype(BF16)


def _ple_split_kernel(xb_ref, wg_ref, p_ref, wp_ref, x_ref, o1_ref, o2_ref, *, n_first):
    gate = jax.nn.sigmoid(_dot(xb_ref[...], wg_ref[...]))
    out = x_ref[...] + gate * _dot(p_ref[...], wp_ref[...])

    @pl.when(pl.program_id(1) < n_first)
    def _():
        o1_ref[...] = out

    @pl.when(pl.program_id(1) >= n_first)
    def _():
        o2_ref[...] = out


def _ple(x, xb, pb, w_gate, w_proj, *, split_rows=None, bm=512, bn=1024):
    m, d = x.shape
    pdim = pb.shape[1]
    tile = pl.BlockSpec((bm, bn), lambda j, i: (i, j))
    if split_rows is None:
        body, out_specs = _ple_kernel, [tile, tile]
        out_shape = [jax.ShapeDtypeStruct((m, d), F32), jax.ShapeDtypeStruct((m, d), BF16)]
    else:
        assert split_rows % bm == 0 and 0 < split_rows < m
        n_first = split_rows // bm
        body = functools.partial(_ple_split_kernel, n_first=n_first)
        out_specs = [pl.BlockSpec((bm, bn), lambda j, i: (jnp.minimum(i, n_first - 1), j)),
                     pl.BlockSpec((bm, bn), lambda j, i: (jnp.maximum(i - n_first, 0), j))]
        out_shape = [jax.ShapeDtypeStruct((split_rows, d), F32), jax.ShapeDtypeStruct((m - split_rows, d), F32)]
    return pl.pallas_call(
        body,
        grid=(d // bn, m // bm),
        in_specs=[pl.BlockSpec((bm, d), lambda j, i: (i, 0)), pl.BlockSpec((d, bn), lambda j, i: (0, j)),
                  pl.BlockSpec((bm, pdim), lambda j, i: (i, 0)), pl.BlockSpec((pdim, bn), lambda j, i: (0, j)),
                  tile],
        out_specs=out_specs,
        out_shape=out_shape,
        compiler_params=_params(("parallel", "arbitrary"), 48),
    )(xb, w_gate, pb, w_proj, x)


def kernel(x_prompt, x_sample, cache_k, cache_v, state_s5_re, state_s5_im, state_lru, state_conv, p_prompt, p_sample, ln_g, ln_b, ffn_w_in, ffn_w_out, w_in, b_gate, s5_lam_re, s5_lam_im, s5_log_step, s5_b_re, s5_b_im, s5_c_re, s5_c_im, s5_d, s5_w_glu, s5_b_glu, diff_lambda, diff_subln, lru_conv_w, lru_conv_b, lru_w_a, lru_b_a, lru_w_x, lru_b_x, lru_lambda, w_br_a, w_br_b, w_br_c, w_o, ple_w_proj, ple_w_gate):
    bp, tp, d_model = x_prompt.shape
    bs, ts, _ = x_sample.shape
    depth = ln_g.shape[0]
    mp, ms = bp * tp, bs * ts
    m = mp + ms
    n_heads, e = cache_k.shape[3], cache_k.shape[4]
    hd = e // 2
    past = cache_k.shape[2]
    groups, p_state, grp_ch = s5_b_re.shape[1:]
    s5_w = groups * grp_ch
    att_w = n_heads * e
    lru_w = lru_lambda.shape[1]
    d_ff = ffn_w_out.shape[2]
    f_pad = -(-d_ff // 1024) * 1024
    alpha = (2 * depth) ** 0.25
    L = S5_CHUNK
    assert ts == L and tp % L == 0 and tp % LRU_ROWS_LONG == 0 and ts % 8 == 0 and ts == CHUNK and past % CHUNK == 0
    kp = tp // L
    w_k, w_v = s5_w + att_w, s5_w + 2 * att_w
    c_q = s5_w
    c_xr = s5_w + att_w
    c_gr = c_xr + lru_w
    c_gl = c_gr + lru_w
    kv_bufs = [jnp.zeros((depth, rows, att_w), F32) for rows in (mp, mp, ms, ms)]

    x = jnp.concatenate([x_prompt.reshape(mp, d_model), x_sample.reshape(ms, d_model)], axis=0)
    xb = x.astype(BF16)
    slopes = jnp.exp2(-8.0 * (jnp.arange(n_heads, dtype=F32) + 1.0) / n_heads)

    w_dn_all = _cast_pad_rows(ffn_w_out.reshape(-1, d_ff, d_model), f_pad).reshape(depth, -1, f_pad, d_model)

    def ffn(i, s, x, xb):
        hdn = _swiglu_in(xb, ffn_w_in, w_lead=(i, s), f_pad=f_pad, bm=1024, bn=256)
        y = _matmul_res(hdn, w_dn_all, x, w_lead=(i, s), bm=1024, bn=1024, bk=f_pad // 4, alpha=alpha, scale=0.5,
                        vmem_mib=48)
        return _layer_norm(y, ln_g[i, 2 * s], ln_b[i, 2 * s])

    s5r, s5i, lruh, convs = [], [], [], []
    for i in range(depth):
        x, xb = ffn(i, 0, x, xb)

        proj = _matmul_f32w(xb, w_in, w_lead=(i,), skip_cols=(w_k, w_v + att_w), bm=1024, bn=512, out_dtype=F32,
                            vmem_mib=48)
        kv_bufs = [_matmul_f32w_slab(xb, w_in, buf, w_lead=(i,), w_col0=col, row0=row0, slab=i, bm=1024, bn=512,
                                     vmem_mib=48)
                   for buf, col, row0 in zip(kv_bufs, (w_k, w_v, w_k, w_v), (0, 0, mp, mp))]
        k_p, v_p, k_s, v_s = kv_bufs

        u_p = proj[:mp, :s5_w].reshape(bp, kp, L, groups, grp_ch).transpose(3, 1, 0, 2, 4).reshape(groups, kp * bp, L * grp_ch)
        u_s = proj[mp:, :s5_w].reshape(bs, L, groups, grp_ch).transpose(2, 0, 1, 3).reshape(groups, bs, L * grp_ch)
        u_g = jnp.concatenate([u_p, u_s], axis=1)
        h0_g = jnp.concatenate([state_s5_re[i], state_s5_im[i]], axis=-1).transpose(1, 0, 2)
        mats = _s5_matrices(s5_lam_re[i], s5_lam_im[i], s5_log_step[i], s5_b_re[i], s5_b_im[i], s5_c_re[i], s5_c_im[i])
        y_g, hfin = _s5_scan(u_g, h0_g, mats, n_seq=bp, n_chunk=kp, n_single=bs)
        y_p = y_g[:, :kp * bp].reshape(groups, kp, bp, L, grp_ch).transpose(2, 1, 3, 0, 4).reshape(mp, s5_w)
        y_s = y_g[:, kp * bp:].reshape(groups, bs, L, grp_ch).transpose(1, 2, 0, 3).reshape(ms, s5_w)
        y_a = _s5_glu(y_p, y_s, proj, s5_d[i], s5_w_glu[i].astype(BF16), s5_b_glu[i])
        hfin = hfin.transpose(1, 0, 2)
        s5r.append((hfin[:bp, :, :p_state], hfin[bp:, :, :p_state]))
        s5i.append((hfin[:bp, :, p_state:], hfin[bp:, :, p_state:]))

        lam_init = 0.8 - 0.6 * math.exp(-0.3 * i)
        dl = diff_lambda[i].astype(F32)
        lam = jnp.exp(jnp.sum(dl[0] * dl[1])) - jnp.exp(jnp.sum(dl[2] * dl[3])) + lam_init
        scal = jnp.concatenate([slopes, lam.reshape(1)])
        y_b = _attn_prompt(proj, k_p, v_p, slopes, lam, diff_subln[i], jnp.zeros((m, att_w), BF16), layer=i,
                           n_batch=bp, seq=tp, n_heads=n_heads, hd=hd, out_scale=1.0 - lam_init, q_col=c_q // e)
        y_b = _attn_sample(proj, k_s, v_s, cache_k, cache_v, scal, diff_subln[i], y_b, layer=i, n_batch=bs, seq=ts,
                           n_heads=n_heads, hd=hd, row0=mp, out_scale=1.0 - lam_init, q_col=c_q // e)

        lru_params = (lru_conv_w[i], lru_conv_b[i], _block_diag(lru_w_a[i]).astype(BF16), lru_b_a[i].reshape(-1),
                      _block_diag(lru_w_x[i]).astype(BF16), lru_b_x[i].reshape(-1),
                      jax.nn.softplus(-lru_lambda[i].astype(F32)))
        lru_cols = dict(x_col=c_xr // lru_w, gate_col=c_gr // lru_w)
        y_c, h_p, c_p = _rglru(proj, jnp.zeros((bp, lru_w), F32), jnp.zeros((bp,) + state_conv.shape[2:], F32),
                               *lru_params, jnp.zeros((m, lru_w), BF16), rows=LRU_ROWS_LONG, row0=0,
                               blocks_per_seq=tp // LRU_ROWS_LONG, **lru_cols)
        y_c, h_s, c_s = _rglru(proj, state_lru[i], state_conv[i], *lru_params, y_c, rows=ts, row0=mp,
                               blocks_per_seq=1, **lru_cols)
        lruh.append((h_p[:, 0], h_s[:, 0]))
        convs.append((c_p, c_s))

        merged = _merge(y_a, y_b, y_c, w_br_a[i].astype(BF16), w_br_b[i].astype(BF16), w_br_c[i].astype(BF16),
                        proj, b_gate[i], gate_col=c_gl // 1024)
        y = _matmul_f32w(merged, w_o, x, w_lead=(i,), bm=1024, bn=512, out_dtype=F32, vmem_mib=48, alpha=alpha)
        x, xb = _layer_norm(y, ln_g[i, 1], ln_b[i, 1])

        x, xb = ffn(i, 1, x, xb)

        pb = jnp.concatenate([p_prompt[i].reshape(mp, -1), p_sample[i].reshape(ms, -1)], axis=0).astype(BF16)
        x, xb = _ple(x, xb, pb, ple_w_gate[i].astype(BF16), ple_w_proj[i].astype(BF16),
                     split_rows=mp if i == depth - 1 else None)
    y_prompt, y_sample = x, xb

    stack = lambda pairs, which: jnp.stack([pr[which] for pr in pairs])
    return (y_prompt.reshape(bp, tp, d_model), y_sample.reshape(bs, ts, d_model),
            k_p.reshape(depth, bp, tp, n_heads, e), v_p.reshape(depth, bp, tp, n_heads, e),
            stack(s5r, 0), stack(s5i, 0), stack(lruh, 0), stack(convs, 0),
            k_s.reshape(depth, bs, ts, n_heads, e), v_s.reshape(depth, bs, ts, n_heads, e),
            stack(s5r, 1), stack(s5i, 1), stack(lruh, 1), stack(convs, 1))
```

```python
import functools
import math

import jax
import jax.numpy as jnp
from jax import lax
from jax.experimental import pallas as pl
from jax.experimental.pallas import tpu as pltpu

F32 = jnp.float32
BF16 = jnp.bfloat16

CHUNK = 64
LRU_C = 8.0
LN_EPS = 1e-5
NEG_INF = -1e30
S5_CHUNK = 64
LRU_ROWS_LONG = 256

LANES = 128
LOG2E = math.log2(math.e)

MIB = 1024 * 1024


def _params(semantics, vmem_mib):
    return pltpu.CompilerParams(dimension_semantics=semantics, vmem_limit_bytes=vmem_mib * MIB)


def _dot(a, b):
    return jnp.dot(a, b, preferred_element_type=F32)


def _dot_nt(a, b):
    return lax.dot_general(a, b, (((1,), (1,)), ((), ())), preferred_element_type=F32)


def _split_bf16(a):
    hi = a.astype(BF16)
    lo = (a - hi.astype(F32)).astype(BF16)
    return hi, lo


def _dot3(a, b):
    a_hi, a_lo = _split_bf16(a)
    b_hi, b_lo = _split_bf16(b)
    return _dot(a_hi, b_hi) + (_dot(a_lo, b_hi) + _dot(a_hi, b_lo))


def _mm_res_kernel(x_ref, w_ref, r_ref, o_ref, *, nk, alpha, scale):
    part = _dot(x_ref[...], w_ref[...])
    k = pl.program_id(2)

    @pl.when(k == 0)
    def _():
        o_ref[...] = part

    @pl.when(k > 0)
    def _():
        o_ref[...] += part

    @pl.when(k == nk - 1)
    def _():
        o_ref[...] = alpha * r_ref[...] + scale * o_ref[...]


def _matmul_res(x, w, res, *, w_lead=(), bm, bn, bk, alpha, scale, vmem_mib):
    m, kdim = x.shape
    n = w.shape[-1]
    nk = kdim // bk
    assert w.shape[-2] == kdim and m % bm == 0 and n % bn == 0 and kdim % bk == 0 and nk > 1
    tile = pl.BlockSpec((bm, bn), lambda i, j, k: (i, j))
    return pl.pallas_call(
        functools.partial(_mm_res_kernel, nk=nk, alpha=alpha, scale=scale),
        grid=(m // bm, n // bn, nk),
        in_specs=[pl.BlockSpec((bm, bk), lambda i, j, k: (i, k)),
                  pl.BlockSpec((None,) * len(w_lead) + (bk, bn), lambda i, j, k: w_lead + (k, j)), tile],
        out_specs=tile,
        out_shape=jax.ShapeDtypeStruct((m, n), F32),
        compiler_params=_params(("parallel", "parallel", "arbitrary"), vmem_mib),
    )(x, w, res)


def _mm_f32w_kernel(x_ref, w_ref, *rest, alpha, scale):
    o_ref, w_sc = rest[-2:]

    @pl.when(pl.program_id(1) == 0)
    def _():
        w_sc[...] = w_ref[...].astype(BF16)

    acc = _dot(x_ref[...], w_sc[...])
    if len(rest) == 3:
        acc = alpha * rest[0][...] + scale * acc
    o_ref[...] = acc.astype(o_ref.dtype)


def _matmul_f32w(x, w, res=None, *, w_lead=(), skip_cols=(0, 0), bm, bn, out_dtype, vmem_mib, alpha=1.0, scale=1.0):
    m, kdim = x.shape
    lo, hi = skip_cols[0] // bn, skip_cols[1] // bn
    n = w.shape[-1] - (hi - lo) * bn
    assert w.shape[-2] == kdim and m % bm == 0 and n % bn == 0 and skip_cols == (lo * bn, hi * bn)
    w_col = lambda j: j + (hi - lo) * (j >= lo).astype(jnp.int32) if hi > lo else j
    tile = pl.BlockSpec((bm, bn), lambda j, i: (i, j))
    extra = [] if res is None else [res]
    return pl.pallas_call(
        functools.partial(_mm_f32w_kernel, alpha=alpha, scale=scale),
        grid=(n // bn, m // bm),
        in_specs=[pl.BlockSpec((bm, kdim), lambda j, i: (i, 0)),
                  pl.BlockSpec((None,) * len(w_lead) + (kdim, bn), lambda j, i: w_lead + (0, w_col(j)))]
                 + [tile] * len(extra),
        out_specs=tile,
        out_shape=jax.ShapeDtypeStruct((m, n), out_dtype),
        scratch_shapes=[pltpu.VMEM((kdim, bn), BF16)],
        compiler_params=_params(("parallel", "arbitrary"), vmem_mib),
    )(x, w, *extra)


def _mm_f32w_slab_kernel(x_ref, w_ref, _, o_ref, w_sc):
    @pl.when(pl.program_id(1) == 0)
    def _():
        w_sc[...] = w_ref[...].astype(BF16)

    o_ref[...] = _dot(x_ref[...], w_sc[...])


def _matmul_f32w_slab(x, w, buf, *, w_lead, w_col0, row0, slab, bm, bn, vmem_mib):
    kdim = x.shape[1]
    _, rows, cols = buf.shape
    assert rows % bm == 0 and cols % bn == 0 and row0 % bm == 0 and w_col0 % bn == 0 and buf.dtype == F32
    return pl.pallas_call(
        _mm_f32w_slab_kernel,
        grid=(cols // bn, rows // bm),
        in_specs=[pl.BlockSpec((bm, kdim), lambda j, i: (row0 // bm + i, 0)),
                  pl.BlockSpec((None,) * len(w_lead) + (kdim, bn), lambda j, i: w_lead + (0, w_col0 // bn + j)),
                  pl.BlockSpec(memory_space=pl.ANY)],
        out_specs=pl.BlockSpec((None, bm, bn), lambda j, i: (slab, i, j)),
        out_shape=jax.ShapeDtypeStruct(buf.shape, F32),
        input_output_aliases={2: 0},
        scratch_shapes=[pltpu.VMEM((kdim, bn), BF16)],
        compiler_params=_params(("parallel", "arbitrary"), vmem_mib),
    )(x, w, buf)


def _cast_pad_kernel(w_ref, o_ref, *, n_real):
    @pl.when(pl.program_id(1) < n_real)
    def _():
        o_ref[...] = w_ref[...].astype(BF16)

    @pl.when(pl.program_id(1) >= n_real)
    def _():
        o_ref[...] = jnp.zeros(o_ref.shape, BF16)


def _cast_pad_rows(w, rows_pad, *, br=256):
    n, rows, cols = w.shape
    assert rows % br == 0 and rows_pad % br == 0
    n_real = rows // br
    return pl.pallas_call(
        functools.partial(_cast_pad_kernel, n_real=n_real),
        grid=(n, rows_pad // br),
        in_specs=[pl.BlockSpec((None, br, cols), lambda a, r: (a, jnp.minimum(r, n_real - 1), 0))],
        out_specs=pl.BlockSpec((None, br, cols), lambda a, r: (a, r, 0)),
        out_shape=jax.ShapeDtypeStruct((n, rows_pad, cols), BF16),
        compiler_params=_params(("parallel", "parallel"), 32),
    )(w)


def _swiglu_kernel(x_ref, wg_ref, wu_ref, o_ref, wg_sc, wu_sc, *, n_real):
    j = pl.program_id(0)

    @pl.when(jnp.logical_and(pl.program_id(1) == 0, j < n_real))
    def _():
        wg_sc[...] = wg_ref[...].astype(BF16)
        wu_sc[...] = wu_ref[...].astype(BF16)

    @pl.when(j < n_real)
    def _():
        x = x_ref[...]
        g = _dot(x, wg_sc[...])
        u = _dot(x, wu_sc[...])
        o_ref[...] = (g * jax.nn.sigmoid(g) * u).astype(o_ref.dtype)

    @pl.when(j >= n_real)
    def _():
        o_ref[...] = jnp.zeros(o_ref.shape, o_ref.dtype)


def _swiglu_in(xb, w_gu, *, w_lead, f_pad, bm, bn):
    m, kdim = xb.shape
    f = w_gu.shape[-1] // 2
    n_real = f // bn
    nj = f_pad // bn
    assert f % bn == 0 and f_pad % bn == 0 and m % bm == 0
    lead = (None,) * len(w_lead)
    col = lambda j: jnp.minimum(j, n_real - 1)
    return pl.pallas_call(
        functools.partial(_swiglu_kernel, n_real=n_real),
        grid=(nj, m // bm),
        in_specs=[pl.BlockSpec((bm, kdim), lambda j, i: (i, 0)),
                  pl.BlockSpec(lead + (kdim, bn), lambda j, i: w_lead + (0, col(j))),
                  pl.BlockSpec(lead + (kdim, bn), lambda j, i: w_lead + (0, col(j) + n_real))],
        out_specs=pl.BlockSpec((bm, bn), lambda j, i: (i, j)),
        out_shape=jax.ShapeDtypeStruct((m, f_pad), BF16),
        scratch_shapes=[pltpu.VMEM((kdim, bn), BF16), pltpu.VMEM((kdim, bn), BF16)],
        compiler_params=_params(("parallel", "arbitrary"), 48),
    )(xb, w_gu, w_gu)


def _join_kernel(x1_ref, x2_ref, o_ref, ob_ref, *, n_first):
    def run(x_ref):
        o_ref[...] = x_ref[...]
        ob_ref[...] = x_ref[...].astype(BF16)

    pl.when(pl.program_id(0) < n_first)(lambda: run(x1_ref))
    pl.when(pl.program_id(0) >= n_first)(lambda: run(x2_ref))


def _join_rows(x1, x2, *, bm=256):
    (m1, d), m2 = x1.shape, x2.shape[0]
    assert m1 % bm == 0 and m2 % bm == 0
    n_first = m1 // bm
    row = pl.BlockSpec((bm, d), lambda i: (i, 0))
    return pl.pallas_call(
        functools.partial(_join_kernel, n_first=n_first),
        grid=((m1 + m2) // bm,),
        in_specs=[pl.BlockSpec((bm, d), lambda i: (jnp.minimum(i, n_first - 1), 0)),
                  pl.BlockSpec((bm, d), lambda i: (jnp.maximum(i - n_first, 0), 0))],
        out_specs=[row, row],
        out_shape=[jax.ShapeDtypeStruct((m1 + m2, d), F32), jax.ShapeDtypeStruct((m1 + m2, d), BF16)],
        compiler_params=_params(("parallel",), 40),
    )(x1, x2)


def _ln_kernel(y_ref, g_ref, b_ref, o_ref, ob_ref):
    y = y_ref[...]
    mu = jnp.mean(y, axis=-1, keepdims=True)
    d = y - mu
    var = jnp.mean(d * d, axis=-1, keepdims=True)
    out = d * lax.rsqrt(var + LN_EPS) * g_ref[...] + b_ref[...]
    o_ref[...] = out
    ob_ref[...] = out.astype(BF16)


def _layer_norm(y, g, b, *, bm=256):
    m, d = y.shape
    row = pl.BlockSpec((bm, d), lambda i: (i, 0))
    vec = pl.BlockSpec((1, d), lambda i: (0, 0))
    return pl.pallas_call(
        _ln_kernel,
        grid=(m // bm,),
        in_specs=[row, vec, vec],
        out_specs=[row, row],
        out_shape=[jax.ShapeDtypeStruct((m, d), F32), jax.ShapeDtypeStruct((m, d), BF16)],
        compiler_params=_params(("parallel",), 40),
    )(y, g.reshape(1, d), b.reshape(1, d))


def _s5_kernel(u_ref, h0_ref, min_ref, minsw_ref, kseq_ref, mout_ref, dec_ref, y_ref, hfin_ref,
               s_sc, ssw_sc, hprev_sc, toep_sc, *, n_seq, n_chunk, n_single):
    ch, seq_w = kseq_ref.shape
    width = toep_sc.shape[1]
    per_tile = LANES // ch
    kseq = kseq_ref[...]
    for b in range(per_tile):
        rot = kseq if b == 0 else pltpu.roll(kseq, seq_w - ch * b, axis=1)
        for a in range(width // LANES):
            sigma = width // ch - 1 - (a * per_tile + b)
            toep_sc[sigma * ch:(sigma + 1) * ch, :] = rot[:, a * LANES:a * LANES + width]

    u = u_ref[...]
    u_hi, u_lo = _split_bf16(u)

    def dot3_u(w):
        w_hi, w_lo = _split_bf16(w)
        return _dot(u_hi, w_hi) + (_dot(u_lo, w_hi) + _dot(u_hi, w_lo))

    s_sc[...] = dot3_u(min_ref[...])
    ssw_sc[...] = dot3_u(minsw_ref[...])
    a1 = dec_ref[0:1, :]
    a2 = dec_ref[1:2, :]
    a2sw = dec_ref[2:3, :]

    h = jnp.zeros((n_seq, s_sc.shape[1]), F32)
    hsw = h
    for k in range(n_chunk):
        rows = slice(k * n_seq, (k + 1) * n_seq)
        hprev_sc[rows, :] = h
        h, hsw = (a1 * h + a2 * hsw + s_sc[rows, :], a1 * hsw + a2sw * h + ssw_sc[rows, :])
    n_chain = n_chunk * n_seq
    hfin_ref[0:n_seq, :] = h
    h0 = h0_ref[...]
    h0sw = pltpu.roll(h0, h0.shape[1] // 2, axis=1)
    hprev_sc[n_chain:n_chain + n_single, :] = h0
    hfin_ref[n_seq:n_seq + n_single, :] = a1 * h0 + a2 * h0sw + s_sc[n_chain:n_chain + n_single, :]

    y_ref[...] = dot3_u(toep_sc[...]) + _dot3(hprev_sc[...], mout_ref[...])


def _s5_scan(u_g, h0_g, mats, *, n_seq, n_chunk, n_single):
    m_in, m_in_sw, kseq, m_out, dec = mats
    g, rows, width = u_g.shape
    p2 = m_in.shape[2]
    n_out = n_seq + n_single
    grp = lambda *shape: pl.BlockSpec((None,) + shape, lambda i: (i,) + (0,) * len(shape))
    return pl.pallas_call(
        functools.partial(_s5_kernel, n_seq=n_seq, n_chunk=n_chunk, n_single=n_single),
        grid=(g,),
        in_specs=[grp(rows, width), grp(n_single, p2), grp(width, p2), grp(width, p2),
                  grp(*kseq.shape[1:]), grp(p2, width), grp(3, p2)],
        out_specs=[grp(rows, width), grp(n_out, p2)],
        out_shape=[jax.ShapeDtypeStruct((g, rows, width), F32),
                   jax.ShapeDtypeStruct((g, n_out, p2), F32)],
        scratch_shapes=[pltpu.VMEM((rows, p2), F32), pltpu.VMEM((rows, p2), F32),
                        pltpu.VMEM((rows, p2), F32), pltpu.VMEM((width, width), F32)],
        compiler_params=_params(("parallel",), 40),
    )(u_g, h0_g, m_in, m_in_sw, kseq, m_out, dec)


def _s5_matrices(lam_re, lam_im, log_step, b_re, b_im, c_re, c_im):
    hp = lax.Precision.HIGHEST
    L = S5_CHUNK
    g, p = lam_re.shape
    ch = b_re.shape[2]
    lam = lax.complex(lam_re.astype(F32), lam_im.astype(F32))
    step = jnp.exp(log_step.astype(F32))[:, None]
    lam_step = lam * step
    lam_bar = jnp.exp(lam_step)
    b_bar = ((lam_bar - 1.0) / lam)[..., None] * lax.complex(b_re.astype(F32), b_im.astype(F32))
    c_mat = lax.complex(c_re.astype(F32), c_im.astype(F32))
    d = jnp.arange(L + 1, dtype=F32)
    pw = jnp.exp(lam_step[None] * d[:, None, None])
    w_in = pw[:L][::-1].transpose(1, 0, 2)[:, :, None, :] * b_bar.transpose(0, 2, 1)[:, None, :, :]
    w_in = w_in.reshape(g, L * ch, p)
    m_in = jnp.concatenate([jnp.real(w_in), jnp.imag(w_in)], axis=-1)
    m_in_sw = jnp.concatenate([jnp.imag(w_in), jnp.real(w_in)], axis=-1)
    w_out = pw[1:].transpose(1, 2, 0)[:, :, :, None] * c_mat.transpose(0, 2, 1)[:, :, None, :]
    w_out = w_out.reshape(g, p, L * ch)
    m_out = jnp.concatenate([jnp.real(w_out), -jnp.imag(w_out)], axis=1)
    cp = c_mat[None] * pw[:L][:, :, None, :]
    kd = jnp.einsum('dgcp,gpe->dgce', jnp.concatenate([jnp.real(cp), -jnp.imag(cp)], axis=-1),
                    jnp.concatenate([jnp.real(b_bar), jnp.imag(b_bar)], axis=1), precision=hp)
    kseq = jnp.pad(kd.transpose(1, 3, 0, 2), ((0, 0), (0, 0), (L - 1, 1), (0, 0))).reshape(g, ch, 2 * L * ch)
    pl_ = pw[L]
    dec = jnp.stack([jnp.concatenate([jnp.real(pl_), jnp.real(pl_)], -1),
                     jnp.concatenate([-jnp.imag(pl_), jnp.imag(pl_)], -1),
                     jnp.concatenate([jnp.imag(pl_), -jnp.imag(pl_)], -1)], axis=1)
    return m_in, m_in_sw, kseq, m_out, dec


def _glu_kernel(y1_ref, y2_ref, u_ref, d_ref, w_ref, b_ref, o_ref, *, n_first):
    def run(y_ref):
        y = jax.nn.gelu(y_ref[...] + d_ref[...] * u_ref[...])
        z = _dot(y.astype(BF16), w_ref[...]) + b_ref[...]
        o_ref[...] = (y * jax.nn.sigmoid(z)).astype(BF16)

    pl.when(pl.program_id(0) < n_first)(lambda: run(y1_ref))
    pl.when(pl.program_id(0) >= n_first)(lambda: run(y2_ref))


def _s5_glu(y1, y2, proj, d_skip, w_glu, b_glu, *, bm=512):
    (m1, w), m2 = y1.shape, y2.shape[0]
    assert m1 % bm == 0 and m2 % bm == 0
    n_first = m1 // bm
    row = pl.BlockSpec((bm, w), lambda i: (i, 0))
    vec = pl.BlockSpec((1, w), lambda i: (0, 0))
    return pl.pallas_call(
        functools.partial(_glu_kernel, n_first=n_first),
        grid=((m1 + m2) // bm,),
        in_specs=[pl.BlockSpec((bm, w), lambda i: (jnp.minimum(i, n_first - 1), 0)),
                  pl.BlockSpec((bm, w), lambda i: (jnp.maximum(i - n_first, 0), 0)),
                  row, vec, pl.BlockSpec((w, w), lambda i: (0, 0)), vec],
        out_specs=row,
        out_shape=jax.ShapeDtypeStruct((m1 + m2, w), BF16),
        compiler_params=_params(("parallel",), 32),
    )(y1, y2, proj, d_skip.reshape(1, w), w_glu, b_glu.reshape(1, w))


def _lru_kernel(x_ref, gate_ref, h0_ref, c0_ref, cw_ref, cb_ref, wa_ref, ba_ref, wx_ref, bx_ref, sp_ref, _,
                y_ref, hout_ref, cout_ref, xpad_sc, a_sc, b_sc, hs_sc, h_sc,
                *, rows, blocks_per_seq, conv_width):
    tail = conv_width - 1
    base = 8

    @pl.when(pl.program_id(0) % blocks_per_seq == 0)
    def _():
        h_sc[...] = h0_ref[...]
        xpad_sc[base - tail:base, :] = c0_ref[...]

    x = x_ref[...]
    xpad_sc[base:base + rows, :] = x
    xc = cb_ref[...]
    for j in range(conv_width):
        off = base - tail + j
        xc = xc + cw_ref[j:j + 1, :] * xpad_sc[off:off + rows, :]
    new_tail = x_ref[rows - tail:rows, :]
    xpad_sc[base - tail:base, :] = new_tail
    cout_ref[...] = new_tail

    xcb = xc.astype(BF16)
    r = jax.nn.sigmoid(_dot(xcb, wa_ref[...]) + ba_ref[...])
    ig = jax.nn.sigmoid(_dot(xcb, wx_ref[...]) + bx_ref[...])
    log_a = -LRU_C * r * sp_ref[...]
    a = jnp.exp(log_a)
    mult = jnp.sqrt(-jnp.tanh(log_a) * (a * a + 1.0))
    a_sc[...] = a
    b_sc[...] = mult * (ig * xc)

    def step(t, h):
        h = a_sc[pl.ds(t, 1), :] * h + b_sc[pl.ds(t, 1), :]
        hs_sc[pl.ds(t, 1), :] = h
        return h

    h = lax.fori_loop(0, rows, step, h_sc[...], unroll=8)
    h_sc[...] = h
    hout_ref[...] = h
    y_ref[...] = (hs_sc[...] * jax.nn.gelu(gate_ref[...])).astype(BF16)


def _rglru(proj, h0, c0, conv_w, conv_b, wa_bd, b_a, wx_bd, b_x, sp, y_buf, *, rows, row0, blocks_per_seq,
           x_col, gate_col):
    n_seq, width = h0.shape
    cw = conv_w.shape[0]
    assert row0 % rows == 0 and y_buf.shape[1] == width and y_buf.dtype == BF16
    blk0 = row0 // rows
    vec = pl.BlockSpec((1, width), lambda i: (0, 0))
    mat = pl.BlockSpec((width, width), lambda i: (0, 0))
    hspec = pl.BlockSpec((None, 1, width), lambda i: (i // blocks_per_seq, 0, 0))
    cspec = pl.BlockSpec((None, cw - 1, width), lambda i: (i // blocks_per_seq, 0, 0))
    return pl.pallas_call(
        functools.partial(_lru_kernel, rows=rows, blocks_per_seq=blocks_per_seq, conv_width=cw),
        grid=(n_seq * blocks_per_seq,),
        in_specs=[pl.BlockSpec((rows, width), lambda i: (blk0 + i, x_col)),
                  pl.BlockSpec((rows, width), lambda i: (blk0 + i, gate_col)),
                  hspec, cspec, pl.BlockSpec((cw, width), lambda i: (0, 0)), vec, mat, vec, mat, vec, vec,
                  pl.BlockSpec(memory_space=pl.ANY)],
        out_specs=[pl.BlockSpec((rows, width), lambda i: (blk0 + i, 0)), hspec, cspec],
        out_shape=[jax.ShapeDtypeStruct(y_buf.shape, BF16),
                   jax.ShapeDtypeStruct((n_seq, 1, width), F32),
                   jax.ShapeDtypeStruct((n_seq, cw - 1, width), F32)],
        input_output_aliases={11: 0},
        scratch_shapes=[pltpu.VMEM((rows + 8, width), F32), pltpu.VMEM((rows, width), F32),
                        pltpu.VMEM((rows, width), F32), pltpu.VMEM((rows, width), F32),
                        pltpu.VMEM((1, width), F32)],
        compiler_params=_params(("arbitrary",), 32),
    )(proj, proj, h0.reshape(n_seq, 1, width), c0, conv_w, conv_b.reshape(1, width), wa_bd, b_a.reshape(1, width),
      wx_bd, b_x.reshape(1, width), sp.reshape(1, width), y_buf)


def _block_diag(w):
    n, c, d = w.shape
    eye = jnp.eye(n, dtype=w.dtype)
    return (eye[:, None, :, None] * w[:, :, None, :]).reshape(n * c, n * d)


def _head_norm(o, g, out_scale):
    o = o * lax.rsqrt(jnp.mean(o * o, axis=-1, keepdims=True) + LN_EPS) * g
    return o * out_scale


def _attn_prompt_kernel(scal_ref, sbt_ref, q_ref, k_ref, v_ref, g_ref, _, o_ref, m_sc, l_sc, acc_sc, doff_sc, ddiag_sc,
                        *, blk, hd, n_heads, out_scale):
    h = pl.program_id(1)
    i = pl.program_id(2)
    nq = pl.num_programs(2)
    slope2 = scal_ref[h]
    lam = scal_ref[n_heads]
    c1 = (hd ** -0.5) * LOG2E
    shift = int(math.log2(CHUNK))
    e = 2 * hd
    sub = blk // 4

    @pl.when(i == 0)
    def _():
        r = lax.broadcasted_iota(jnp.int32, (blk, blk), 0)
        c = lax.broadcasted_iota(jnp.int32, (blk, blk), 1)
        rel = r - c
        doff_sc[...] = slope2 * rel.astype(F32)
        visible = lax.shift_right_arithmetic(c, shift) <= lax.shift_right_arithmetic(r, shift)
        ddiag_sc[...] = jnp.where(visible, slope2 * jnp.abs(rel).astype(F32), -NEG_INF)

    q = q_ref[...].astype(BF16)
    m_sc[...] = jnp.full(m_sc.shape, NEG_INF, F32)
    l_sc[...] = jnp.zeros(l_sc.shape, F32)
    acc_sc[...] = jnp.zeros(acc_sc.shape, F32)

    def process(j, d_ref, sb):
        start = pl.multiple_of(j * blk, blk)
        kj = k_ref[pl.ds(start, blk), :].astype(BF16)
        vj = v_ref[pl.ds(start, blk), :].astype(BF16)
        for c, r in [(c, r) for r in range(blk // sub) for c in range(2)]:
            cols = slice(c * hd, (c + 1) * hd)
            rows = slice(r * sub, (r + 1) * sub)
            t = _dot_nt(q[rows, cols], kj[:, cols]) * c1 - d_ref[rows, :]
            m_prev = m_sc[c, rows, :]
            m_next = jnp.maximum(m_prev, jnp.max(t, axis=1, keepdims=True) - sb)
            p = jnp.exp2(t - jnp.tile(m_next + sb, (1, blk // LANES)))
            corr = jnp.exp2(m_prev - m_next)
            psum = p[:, 0:LANES]
            for w in range(1, blk // LANES):
                psum = psum + p[:, w * LANES:(w + 1) * LANES]
            l_sc[c, rows, :] = corr * l_sc[c, rows, :] + psum
            acc_sc[c, rows, :] = acc_sc[c, rows, :] * jnp.tile(corr, (1, e // LANES)) + _dot(p.astype(BF16), vj)
            m_sc[c, rows, :] = m_next

    def body(j, carry):
        process(j, doff_sc, sbt_ref[h * nq + (i - j)])
        return carry

    lax.fori_loop(0, i, body, 0)
    process(i, ddiag_sc, 0.0)
    outs = [acc_sc[c] / jnp.sum(l_sc[c], axis=1, keepdims=True) for c in range(2)]
    o = outs[0] - lam * outs[1]
    o_ref[...] = _head_norm(o, g_ref[...], out_scale).astype(BF16)


def _attn_prompt(proj, k_all, v_all, slopes, lam, subln, y_buf, *, layer, n_batch, seq, n_heads, hd, q_col,
                 out_scale, blk=512):
    assert blk % CHUNK == 0 and blk % LANES == 0 and seq % blk == 0
    assert y_buf.shape[1] == n_heads * 2 * hd and y_buf.dtype == BF16
    nq = seq // blk
    e = 2 * hd
    scal = jnp.concatenate([slopes * LOG2E, lam.reshape(1)])
    sb_tab = (slopes[:, None] * (LOG2E * blk * jnp.arange(nq, dtype=F32))[None, :]).reshape(-1)
    smem = pl.BlockSpec(memory_space=pltpu.SMEM)
    return pl.pallas_call(
        functools.partial(_attn_prompt_kernel, blk=blk, hd=hd, n_heads=n_heads, out_scale=out_scale),
        grid=(n_batch, n_heads, nq),
        in_specs=[smem, smem,
                  pl.BlockSpec((blk, e), lambda b, h, i: (b * nq + i, q_col + h)),
                  pl.BlockSpec((None, seq, e), lambda b, h, i: (layer, b, h)),
                  pl.BlockSpec((None, seq, e), lambda b, h, i: (layer, b, h)),
                  pl.BlockSpec((1, e), lambda b, h, i: (0, 0)), pl.BlockSpec(memory_space=pl.ANY)],
        out_specs=pl.BlockSpec((blk, e), lambda b, h, i: (b * nq + i, h)),
        out_shape=jax.ShapeDtypeStruct(y_buf.shape, BF16),
        input_output_aliases={6: 0},
        scratch_shapes=[pltpu.VMEM((2, blk, LANES), F32), pltpu.VMEM((2, blk, LANES), F32),
                        pltpu.VMEM((2, blk, e), F32), pltpu.VMEM((blk, blk), F32), pltpu.VMEM((blk, blk), F32)],
        compiler_params=_params(("parallel", "parallel", "arbitrary"), 48),
    )(scal, sb_tab, proj, k_all, v_all, subln.reshape(1, e), y_buf)


def _attn_sample_kernel(scal_ref, q_ref, kn_ref, vn_ref, g_ref, kc_hbm, vc_hbm, _, o_ref, kbuf, vbuf, sem,
                        *, layer, hd, n_heads, past, out_scale):
    b = pl.program_id(0)
    h = pl.program_id(1)
    step = b * n_heads + h
    n_steps = pl.num_programs(0) * n_heads
    slot = lax.rem(step, 2)

    def cache_copies(at_step, at_slot):
        bb = at_step // n_heads
        hh = lax.rem(at_step, n_heads)
        return (pltpu.make_async_copy(kc_hbm.at[layer, bb, :, hh, :], kbuf.at[at_slot], sem.at[0, at_slot]),
                pltpu.make_async_copy(vc_hbm.at[layer, bb, :, hh, :], vbuf.at[at_slot], sem.at[1, at_slot]))

    @pl.when(step == 0)
    def _():
        for cp in cache_copies(step, slot):
            cp.start()

    @pl.when(step + 1 < n_steps)
    def _():
        for cp in cache_copies(step + 1, 1 - slot):
            cp.start()

    slope = scal_ref[h]
    lam = scal_ref[n_heads]
    scale = hd ** -0.5
    shift = int(math.log2(CHUNK))
    t = q_ref.shape[0]
    q = q_ref[...].astype(BF16)
    kn = kn_ref[...].astype(BF16)
    vn = vn_ref[...].astype(BF16)
    qpos = past + lax.broadcasted_iota(jnp.int32, (t, 1), 0)
    bias_c = slope * (qpos - lax.broadcasted_iota(jnp.int32, (1, past), 1)).astype(F32)
    kposn = past + lax.broadcasted_iota(jnp.int32, (1, t), 1)
    bias_n = slope * jnp.abs(qpos - kposn).astype(F32)
    vis_n = lax.shift_right_arithmetic(kposn, shift) <= lax.shift_right_arithmetic(qpos, shift)

    for cp in cache_copies(step, slot):
        cp.wait()
    kc = kbuf[slot].astype(BF16)
    vc = vbuf[slot].astype(BF16)
    outs = []
    for c in range(2):
        cols = slice(c * hd, (c + 1) * hd)
        s_c = _dot_nt(q[:, cols], kc[:, cols]) * scale - bias_c
        s_n = jnp.where(vis_n, _dot_nt(q[:, cols], kn[:, cols]) * scale - bias_n, NEG_INF)
        m = jnp.maximum(jnp.max(s_c, axis=-1, keepdims=True), jnp.max(s_n, axis=-1, keepdims=True))
        p_c = jnp.exp(s_c - m)
        p_n = jnp.exp(s_n - m)
        l = jnp.sum(p_c, axis=-1, keepdims=True) + jnp.sum(p_n, axis=-1, keepdims=True)
        outs.append((_dot(p_c.astype(BF16), vc) + _dot(p_n.astype(BF16), vn)) / l)
    o = outs[0] - lam * outs[1]
    o_ref[...] = _head_norm(o, g_ref[...], out_scale).astype(BF16)


def _attn_sample(proj, k_new, v_new, cache_k, cache_v, scal, subln, y_buf, *, layer, n_batch, seq, n_heads, hd,
                 q_col, row0, out_scale):
    e = 2 * hd
    past = cache_k.shape[2]
    assert y_buf.shape[1] == n_heads * e and y_buf.dtype == BF16 and row0 % seq == 0
    rb0 = row0 // seq
    new = pl.BlockSpec((None, seq, e), lambda b, h: (layer, b, h))
    hbm = pl.BlockSpec(memory_space=pl.ANY)
    return pl.pallas_call(
        functools.partial(_attn_sample_kernel, layer=layer, hd=hd, n_heads=n_heads, past=past, out_scale=out_scale),
        grid=(n_batch, n_heads),
        in_specs=[pl.BlockSpec(memory_space=pltpu.SMEM), pl.BlockSpec((seq, e), lambda b, h: (rb0 + b, q_col + h)),
                  new, new, pl.BlockSpec((1, e), lambda b, h: (0, 0)), hbm, hbm, hbm],
        out_specs=pl.BlockSpec((seq, e), lambda b, h: (rb0 + b, h)),
        out_shape=jax.ShapeDtypeStruct(y_buf.shape, BF16),
        input_output_aliases={7: 0},
        scratch_shapes=[pltpu.VMEM((2, past, e), F32), pltpu.VMEM((2, past, e), F32),
                        pltpu.SemaphoreType.DMA((2, 2))],
        compiler_params=_params(("arbitrary", "arbitrary"), 40),
    )(scal, proj, k_new, v_new, subln.reshape(1, e), cache_k, cache_v, y_buf)


def _merge_kernel(ya_ref, yb_ref, yc_ref, wa_ref, wb_ref, wc_ref, ga_ref, gb_ref, gc_ref, bg_ref, o_ref):
    def branch(y_ref, w_ref, g_ref, r):
        return jax.nn.sigmoid(g_ref[...] + bg_ref[r:r + 1, :]) * _dot(y_ref[...], w_ref[...])

    merged = branch(ya_ref, wa_ref, ga_ref, 0) + branch(yb_ref, wb_ref, gb_ref, 1) + branch(yc_ref, wc_ref, gc_ref, 2)
    o_ref[...] = merged.astype(BF16)


def _merge(ya, yb, yc, wa, wb, wc, proj, b_gate, *, gate_col, bm=512, bn=1024):
    m = ya.shape[0]
    d = wa.shape[1]
    nj = d // bn
    y_spec = lambda y: pl.BlockSpec((bm, y.shape[1]), lambda j, i: (i, 0))
    w_spec = lambda w: pl.BlockSpec((w.shape[0], bn), lambda j, i: (0, j))
    g_spec = lambda r: pl.BlockSpec((bm, bn), lambda j, i: (i, gate_col + r * nj + j))
    return pl.pallas_call(
        _merge_kernel,
        grid=(nj, m // bm),
        in_specs=[y_spec(ya), y_spec(yb), y_spec(yc), w_spec(wa), w_spec(wb), w_spec(wc),
                  g_spec(0), g_spec(1), g_spec(2), pl.BlockSpec((3, bn), lambda j, i: (0, j))],
        out_specs=pl.BlockSpec((bm, bn), lambda j, i: (i, j)),
        out_shape=jax.ShapeDtypeStruct((m, d), BF16),
        compiler_params=_params(("parallel", "parallel"), 48),
    )(ya, yb, yc, wa, wb, wc, proj, proj, proj, b_gate)


def _ple_kernel(xb_ref, wg_ref, p_ref, wp_ref, x_ref, o_ref, ob_ref):
    gate = jax.nn.sigmoid(_dot(xb_ref[...], wg_ref[...]))
    out = x_ref[...] + gate * _dot(p_ref[...], wp_ref[...])
    o_ref[...] = out
    ob_ref[...] = out.astype(BF16)


def _ple_split_kernel(xb_ref, wg_ref, p_ref, wp_ref, x_ref, o1_ref, o2_ref, *, n_first):
    gate = jax.nn.sigmoid(_dot(xb_ref[...], wg_ref[...]))
    out = x_ref[...] + gate * _dot(p_ref[...], wp_ref[...])

    @pl.when(pl.program_id(1) < n_first)
    def _():
        o1_ref[...] = out

    @pl.when(pl.program_id(1) >= n_first)
    def _():
        o2_ref[...] = out


def _ple(x, xb, pb, w_gate, w_proj, *, split_rows=None, bm=512, bn=1024):
    m, d = x.shape
    pdim = pb.shape[1]
    tile = pl.BlockSpec((bm, bn), lambda j, i: (i, j))
    if split_rows is None:
        body, out_specs = _ple_kernel, [tile, tile]
        out_shape = [jax.ShapeDtypeStruct((m, d), F32), jax.ShapeDtypeStruct((m, d), BF16)]
    else:
        assert split_rows % bm == 0 and 0 < split_rows < m
        n_first = split_rows // bm
        body = functools.partial(_ple_split_kernel, n_first=n_first)
        out_specs = [pl.BlockSpec((bm, bn), lambda j, i: (jnp.minimum(i, n_first - 1), j)),
                     pl.BlockSpec((bm, bn), lambda j, i: (jnp.maximum(i - n_first, 0), j))]
        out_shape = [jax.ShapeDtypeStruct((split_rows, d), F32), jax.ShapeDtypeStruct((m - split_rows, d), F32)]
    return pl.pallas_call(
        body,
        grid=(d // bn, m // bm),
        in_specs=[pl.BlockSpec((bm, d), lambda j, i: (i, 0)), pl.BlockSpec((d, bn), lambda j, i: (0, j)),
                  pl.BlockSpec((bm, pdim), lambda j, i: (i, 0)), pl.BlockSpec((pdim, bn), lambda j, i: (0, j)),
                  tile],
        out_specs=out_specs,
        out_shape=out_shape,
        compiler_params=_params(("parallel", "arbitrary"), 48),
    )(xb, w_gate, pb, w_proj, x)


def kernel(x_prompt, x_sample, cache_k, cache_v, state_s5_re, state_s5_im, state_lru, state_conv, p_prompt, p_sample, ln_g, ln_b, ffn_w_in, ffn_w_out, w_in, b_gate, s5_lam_re, s5_lam_im, s5_log_step, s5_b_re, s5_b_im, s5_c_re, s5_c_im, s5_d, s5_w_glu, s5_b_glu, diff_lambda, diff_subln, lru_conv_w, lru_conv_b, lru_w_a, lru_b_a, lru_w_x, lru_b_x, lru_lambda, w_br_a, w_br_b, w_br_c, w_o, ple_w_proj, ple_w_gate):
    bp, tp, d_model = x_prompt.shape
    bs, ts, _ = x_sample.shape
    depth = ln_g.shape[0]
    mp, ms = bp * tp, bs * ts
    m = mp + ms
    n_heads, e = cache_k.shape[3], cache_k.shape[4]
    hd = e // 2
    past = cache_k.shape[2]
    groups, p_state, grp_ch = s5_b_re.shape[1:]
    s5_w = groups * grp_ch
    att_w = n_heads * e
    lru_w = lru_lambda.shape[1]
    d_ff = ffn_w_out.shape[2]
    f_pad = -(-d_ff // 1024) * 1024
    alpha = (2 * depth) ** 0.25
    L = S5_CHUNK
    assert ts == L and tp % L == 0 and tp % LRU_ROWS_LONG == 0 and ts % 8 == 0 and ts == CHUNK and past % CHUNK == 0
    kp = tp // L
    w_k, w_v = s5_w + att_w, s5_w + 2 * att_w
    c_q = s5_w
    c_xr = s5_w + att_w
    c_gr = c_xr + lru_w
    c_gl = c_gr + lru_w
    kv_bufs = [jnp.zeros((depth, rows, att_w), F32) for rows in (mp, mp, ms, ms)]

    x, xb = _join_rows(x_prompt.reshape(mp, d_model), x_sample.reshape(ms, d_model))
    slopes = jnp.exp2(-8.0 * (jnp.arange(n_heads, dtype=F32) + 1.0) / n_heads)

    w_dn_all = _cast_pad_rows(ffn_w_out.reshape(-1, d_ff, d_model), f_pad).reshape(depth, -1, f_pad, d_model)

    def ffn(i, s, x, xb):
        hdn = _swiglu_in(xb, ffn_w_in, w_lead=(i, s), f_pad=f_pad, bm=1024, bn=256)
        y = _matmul_res(hdn, w_dn_all, x, w_lead=(i, s), bm=1024, bn=1024, bk=f_pad // 4, alpha=alpha, scale=0.5,
                        vmem_mib=48)
        return _layer_norm(y, ln_g[i, 2 * s], ln_b[i, 2 * s])

    s5r, s5i, lruh, convs = [], [], [], []
    for i in range(depth):
        x, xb = ffn(i, 0, x, xb)

        proj = _matmul_f32w(xb, w_in, w_lead=(i,), skip_cols=(w_k, w_v + att_w), bm=1024, bn=512, out_dtype=F32,
                            vmem_mib=48)
        kv_bufs = [_matmul_f32w_slab(xb, w_in, buf, w_lead=(i,), w_col0=col, row0=row0, slab=i, bm=1024, bn=512,
                                     vmem_mib=48)
                   for buf, col, row0 in zip(kv_bufs, (w_k, w_v, w_k, w_v), (0, 0, mp, mp))]
        k_p, v_p, k_s, v_s = kv_bufs

        u_p = proj[:mp, :s5_w].reshape(bp, kp, L, groups, grp_ch).transpose(3, 1, 0, 2, 4).reshape(groups, kp * bp, L * grp_ch)
        u_s = proj[mp:, :s5_w].reshape(bs, L, groups, grp_ch).transpose(2, 0, 1, 3).reshape(groups, bs, L * grp_ch)
        u_g = jnp.concatenate([u_p, u_s], axis=1)
        h0_g = jnp.concatenate([state_s5_re[i], state_s5_im[i]], axis=-1).transpose(1, 0, 2)
        mats = _s5_matrices(s5_lam_re[i], s5_lam_im[i], s5_log_step[i], s5_b_re[i], s5_b_im[i], s5_c_re[i], s5_c_im[i])
        y_g, hfin = _s5_scan(u_g, h0_g, mats, n_seq=bp, n_chunk=kp, n_single=bs)
        y_p = y_g[:, :kp * bp].reshape(groups, kp, bp, L, grp_ch).transpose(2, 1, 3, 0, 4).reshape(mp, s5_w)
        y_s = y_g[:, kp * bp:].reshape(groups, bs, L, grp_ch).transpose(1, 2, 0, 3).reshape(ms, s5_w)
        y_a = _s5_glu(y_p, y_s, proj, s5_d[i], s5_w_glu[i].astype(BF16), s5_b_glu[i])
        hfin = hfin.transpose(1, 0, 2)
        s5r.append((hfin[:bp, :, :p_state], hfin[bp:, :, :p_state]))
        s5i.append((hfin[:bp, :, p_state:], hfin[bp:, :, p_state:]))

        lam_init = 0.8 - 0.6 * math.exp(-0.3 * i)
        dl = diff_lambda[i].astype(F32)
        lam = jnp.exp(jnp.sum(dl[0] * dl[1])) - jnp.exp(jnp.sum(dl[2] * dl[3])) + lam_init
        scal = jnp.concatenate([slopes, lam.reshape(1)])
        y_b = _attn_prompt(proj, k_p, v_p, slopes, lam, diff_subln[i], jnp.zeros((m, att_w), BF16), layer=i,
                           n_batch=bp, seq=tp, n_heads=n_heads, hd=hd, out_scale=1.0 - lam_init, q_col=c_q // e)
        y_b = _attn_sample(proj, k_s, v_s, cache_k, cache_v, scal, diff_subln[i], y_b, layer=i, n_batch=bs, seq=ts,
                           n_heads=n_heads, hd=hd, row0=mp, out_scale=1.0 - lam_init, q_col=c_q // e)

        lru_params = (lru_conv_w[i], lru_conv_b[i], _block_diag(lru_w_a[i]).astype(BF16), lru_b_a[i].reshape(-1),
                      _block_diag(lru_w_x[i]).astype(BF16), lru_b_x[i].reshape(-1),
                      jax.nn.softplus(-lru_lambda[i].astype(F32)))
        lru_cols = dict(x_col=c_xr // lru_w, gate_col=c_gr // lru_w)
        y_c, h_p, c_p = _rglru(proj, jnp.zeros((bp, lru_w), F32), jnp.zeros((bp,) + state_conv.shape[2:], F32),
                               *lru_params, jnp.zeros((m, lru_w), BF16), rows=LRU_ROWS_LONG, row0=0,
                               blocks_per_seq=tp // LRU_ROWS_LONG, **lru_cols)
        y_c, h_s, c_s = _rglru(proj, state_lru[i], state_conv[i], *lru_params, y_c, rows=ts, row0=mp,
                               blocks_per_seq=1, **lru_cols)
        lruh.append((h_p[:, 0], h_s[:, 0]))
        convs.append((c_p, c_s))

        merged = _merge(y_a, y_b, y_c, w_br_a[i].astype(BF16), w_br_b[i].astype(BF16), w_br_c[i].astype(BF16),
                        proj, b_gate[i], gate_col=c_gl // 1024)
        y = _matmul_f32w(merged, w_o, x, w_lead=(i,), bm=1024, bn=512, out_dtype=F32, vmem_mib=48, alpha=alpha)
        x, xb = _layer_norm(y, ln_g[i, 1], ln_b[i, 1])

        x, xb = ffn(i, 1, x, xb)

        pb = jnp.concatenate([p_prompt[i].reshape(mp, -1), p_sample[i].reshape(ms, -1)], axis=0).astype(BF16)
        x, xb = _ple(x, xb, pb, ple_w_gate[i].astype(BF16), ple_w_proj[i].astype(BF16),
                     split_rows=mp if i == depth - 1 else None)
    y_prompt, y_sample = x, xb

    stack = lambda pairs, which: jnp.stack([pr[which] for pr in pairs])
    return (y_prompt.reshape(bp, tp, d_model), y_sample.reshape(bs, ts, d_model),
            k_p.reshape(depth, bp, tp, n_heads, e), v_p.reshape(depth, bp, tp, n_heads, e),
            stack(s5r, 0), stack(s5i, 0), stack(lruh, 0), stack(convs, 0),
            k_s.reshape(depth, bs, ts, n_heads, e), v_s.reshape(depth, bs, ts, n_heads, e),
            stack(s5r, 1), stack(s5i, 1), stack(lruh, 1), stack(convs, 1))
```
